```python
import math
import jax, jax.numpy as jnp
from jax import lax
import numpy as np

D_MODEL = 1024
BATCH = 8
SEQ = 4096
DEPTH = 1

CHUNK = 64
D_MIX = D_MODEL
SSD_WIDTH = D_MIX // 2
SSD_HEADS = 8
SSD_HEAD_DIM = SSD_WIDTH // SSD_HEADS
SSD_GROUPS = 2
SSD_STATE = 128
SSD_CONV = 4
SSD_XBC = SSD_WIDTH + 2 * SSD_GROUPS * SSD_STATE
CONF_WIDTH = D_MIX - SSD_WIDTH
CONF_CONV = 31
IN_WIDTH = SSD_WIDTH + SSD_XBC + SSD_HEADS + 2 * CONF_WIDTH
PEER_HEADS = 8
PEER_N_KEYS = 128
PEER_N_EXPERTS = PEER_N_KEYS * PEER_N_KEYS
PEER_D_QUERY = 256
PEER_D_HALF = PEER_D_QUERY // 2
PEER_TOPK = 16
TOKEN_BLOCK = 128
NORM_EPS = 1e-6

kernel_name = 'hybrid_ssd_conformer_peer_block'


def rmsnorm(x, g):
    xf = x.astype(jnp.float32)
    y = xf * lax.rsqrt(jnp.mean(xf * xf, axis=-1, keepdims=True) + NORM_EPS)
    return (y * g.astype(jnp.float32)).astype(x.dtype)


def causal_dwconv(x, w, b):
    k = w.shape[0]
    y = lax.conv_general_dilated(x, w[:, None, :].astype(x.dtype), window_strides=(1,),
                                 padding=[(k - 1, 0)],
                                 dimension_numbers=('NWC', 'WIO', 'NWC'),
                                 feature_group_count=x.shape[-1])
    return y + b.astype(x.dtype)


def ssd_chunked(x, dt, a, bm, cm):
    bsz, s, nh, p = x.shape
    nc = s // CHUNK
    x = x.reshape(bsz, nc, CHUNK, nh, p)
    bm = bm.reshape(bsz, nc, CHUNK, nh, -1)
    cm = cm.reshape(bsz, nc, CHUNK, nh, -1)
    dt = dt.reshape(bsz, nc, CHUNK, nh)
    a_cs = jnp.cumsum(dt * a, axis=2)
    mask = jnp.tril(jnp.ones((CHUNK, CHUNK), dtype=bool))[None, None, :, :, None]
    seg = a_cs[:, :, :, None, :] - a_cs[:, :, None, :, :]
    decay = jnp.exp(jnp.where(mask, seg, -jnp.inf))
    scores = jnp.einsum('bcthn,bcshn->bctsh', cm, bm) * decay * dt[:, :, None, :, :]
    y_diag = jnp.einsum('bctsh,bcshp->bcthp', scores, x)
    to_end = jnp.exp(a_cs[:, :, -1:, :] - a_cs) * dt
    states = jnp.einsum('bcshn,bcsh,bcshp->bchpn', bm, to_end, x)
    chunk_decay = jnp.exp(a_cs[:, :, -1, :])

    def step(h, inp):
        dec, st = inp
        return dec[:, :, None, None] * h + st, h

    h0 = jnp.zeros((bsz, nh, p, bm.shape[-1]), jnp.float32)
    _, h_in = lax.scan(step, h0, (jnp.moveaxis(chunk_decay, 1, 0), jnp.moveaxis(states, 1, 0)))
    h_in = jnp.moveaxis(h_in, 0, 1)
    y_off = jnp.einsum('bcthn,bchpn->bcthp', cm, h_in) * jnp.exp(a_cs)[..., None]
    return (y_diag + y_off).reshape(bsz, s, nh, p)


def ssd_branch(z, xbc, dt_raw, conv_w, conv_b, dt_bias, a_log, d_skip, norm_g):
    bsz, s, _ = z.shape
    f32 = jnp.float32
    xbc = jax.nn.silu(causal_dwconv(xbc, conv_w, conv_b))
    xs, b_in, c_in = jnp.split(xbc, [SSD_WIDTH, SSD_WIDTH + SSD_GROUPS * SSD_STATE], axis=-1)
    rep = SSD_HEADS // SSD_GROUPS
    xh = xs.reshape(bsz, s, SSD_HEADS, SSD_HEAD_DIM).astype(f32)
    bh = jnp.repeat(b_in.reshape(bsz, s, SSD_GROUPS, SSD_STATE), rep, axis=2).astype(f32)
    ch = jnp.repeat(c_in.reshape(bsz, s, SSD_GROUPS, SSD_STATE), rep, axis=2).astype(f32)
    dt = jax.nn.softplus(dt_raw.astype(f32) + dt_bias.astype(f32))
    a = -jnp.exp(a_log.astype(f32))
    y = ssd_chunked(xh, dt, a, bh, ch) + d_skip.astype(f32)[:, None] * xh
    y = y.reshape(bsz, s, SSD_WIDTH) * jax.nn.silu(z.astype(f32))
    return rmsnorm(y, norm_g).astype(z.dtype)


def conformer_conv_branch(glu_in, dw_w, dw_b, ln_g, ln_b):
    val, gate = jnp.split(glu_in, 2, axis=-1)
    u = causal_dwconv(val * jax.nn.sigmoid(gate), dw_w, dw_b)
    uf = u.astype(jnp.float32)
    mu = jnp.mean(uf, axis=-1, keepdims=True)
    var = jnp.mean(jnp.square(uf - mu), axis=-1, keepdims=True)
    un = (uf - mu) * lax.rsqrt(var + NORM_EPS) * ln_g.astype(jnp.float32) + ln_b.astype(jnp.float32)
    return jax.nn.silu(un).astype(u.dtype)


def peer_ffn(xn, w_query, sub_keys, expert_u, expert_v):
    bsz, s, d = xn.shape
    xt = xn.reshape(bsz * s // TOKEN_BLOCK, TOKEN_BLOCK, d)

    def block(xb):
        tb = xb.shape[0]
        q = (xb @ w_query).reshape(tb, PEER_HEADS, 2, PEER_D_HALF)
        sc = jnp.einsum('thid,hikd->thik', q, sub_keys).astype(jnp.float32)
        sv, si = lax.top_k(sc, PEER_TOPK)
        cand = (sv[:, :, 0, :, None] + sv[:, :, 1, None, :]).reshape(tb, PEER_HEADS, -1)
        cand_idx = (si[:, :, 0, :, None] * PEER_N_KEYS + si[:, :, 1, None, :]).reshape(tb, PEER_HEADS, -1)
        best, pos = lax.top_k(cand, PEER_TOPK)
        idx = jnp.take_along_axis(cand_idx, pos, axis=-1)
        g = jax.nn.softmax(best, axis=-1)
        u = expert_u[idx]
        v = expert_v[idx]
        act = jax.nn.gelu(jnp.einsum('td,thkd->thk', xb, u).astype(jnp.float32), approximate=False)
        return jnp.einsum('thk,thkd->td', (g * act).astype(xb.dtype), v)

    return lax.map(block, xt).reshape(bsz, s, d)


def setup_inputs(seed: int = 0) -> dict:
    key = jax.random.key(seed)
    ks = jax.random.split(key, 24)
    f32 = jnp.float32

    def nrm(k, shape, scale):
        return jax.random.normal(k, shape, f32) * scale

    dt0 = jnp.exp(jax.random.uniform(ks[8], (DEPTH, SSD_HEADS), f32) * (math.log(0.1) - math.log(0.001)) + math.log(0.001))
    return {
        'x': nrm(ks[0], (BATCH, SEQ, D_MODEL), 1.0),
        'c': nrm(ks[1], (BATCH, D_MODEL), 1.0),
        'ada_w': nrm(ks[2], (DEPTH, D_MODEL, 6 * D_MODEL), 0.5 * D_MODEL ** -0.5),
        'ada_b': nrm(ks[3], (DEPTH, 6 * D_MODEL), 0.02),
        'norm1_g': 1.0 + nrm(ks[4], (DEPTH, D_MODEL), 0.02),
        'w_in': nrm(ks[5], (DEPTH, D_MODEL, IN_WIDTH), D_MODEL ** -0.5),
        'ssd_conv_w': nrm(ks[6], (DEPTH, SSD_CONV, SSD_XBC), SSD_CONV ** -0.5),
        'ssd_conv_b': nrm(ks[7], (DEPTH, SSD_XBC), 0.02),
        'ssd_dt_bias': dt0 + jnp.log(-jnp.expm1(-dt0)),
        'ssd_a_log': jnp.log(jax.random.uniform(ks[9], (DEPTH, SSD_HEADS), f32, minval=1.0, maxval=16.0)),
        'ssd_d': 1.0 + nrm(ks[10], (DEPTH, SSD_HEADS), 0.02),
        'ssd_norm_g': 1.0 + nrm(ks[11], (DEPTH, SSD_WIDTH), 0.02),
        'conf_dw_w': nrm(ks[12], (DEPTH, CONF_CONV, CONF_WIDTH), CONF_CONV ** -0.5),
        'conf_dw_b': nrm(ks[13], (DEPTH, CONF_WIDTH), 0.02),
        'conf_ln_g': 1.0 + nrm(ks[14], (DEPTH, CONF_WIDTH), 0.02),
        'conf_ln_b': nrm(ks[15], (DEPTH, CONF_WIDTH), 0.02),
        'w_out': nrm(ks[16], (DEPTH, D_MIX, D_MODEL), D_MIX ** -0.5),
        'norm2_g': 1.0 + nrm(ks[17], (DEPTH, D_MODEL), 0.02),
        'peer_w_query': nrm(ks[18], (DEPTH, D_MODEL, PEER_HEADS * PEER_D_QUERY), D_MODEL ** -0.5),
        'peer_sub_keys': nrm(ks[19], (DEPTH, PEER_HEADS, 2, PEER_N_KEYS, PEER_D_HALF), PEER_D_HALF ** -0.5),
        'peer_u': nrm(ks[20], (DEPTH, PEER_N_EXPERTS, D_MODEL), D_MODEL ** -0.5),
        'peer_v': nrm(ks[21], (DEPTH, PEER_N_EXPERTS, D_MODEL), D_MODEL ** -0.5),
        'final_norm_g': 1.0 + nrm(ks[22], (D_MODEL,), 0.02),
    }


def reference(x, c, ada_w, ada_b, norm1_g, w_in, ssd_conv_w, ssd_conv_b, ssd_dt_bias, ssd_a_log,
              ssd_d, ssd_norm_g, conf_dw_w, conf_dw_b, conf_ln_g, conf_ln_b, w_out, norm2_g,
              peer_w_query, peer_sub_keys, peer_u, peer_v, final_norm_g):
    h = x
    cond = jax.nn.silu(c)
    for l in range(DEPTH):
        mod = cond @ ada_w[l] + ada_b[l]
        sh1, sc1, g1, sh2, sc2, g2 = [m[:, None, :] for m in jnp.split(mod, 6, axis=-1)]
        hn = rmsnorm(h, norm1_g[l]) * (1.0 + sc1) + sh1
        proj = hn @ w_in[l]
        z, xbc, dt_raw, glu_in = jnp.split(
            proj, [SSD_WIDTH, SSD_WIDTH + SSD_XBC, SSD_WIDTH + SSD_XBC + SSD_HEADS], axis=-1)
        y_ssd = ssd_branch(z, xbc, dt_raw, ssd_conv_w[l], ssd_conv_b[l], ssd_dt_bias[l],
                           ssd_a_log[l], ssd_d[l], ssd_norm_g[l])
        y_conf = conformer_conv_branch(glu_in, conf_dw_w[l], conf_dw_b[l],
                                       conf_ln_g[l], conf_ln_b[l])
        h = h + g1 * (jnp.concatenate([y_ssd, y_conf], axis=-1) @ w_out[l])
        hn = rmsnorm(h, norm2_g[l]) * (1.0 + sc2) + sh2
        h = h + g2 * peer_ffn(hn, peer_w_query[l], peer_sub_keys[l], peer_u[l], peer_v[l])
    return rmsnorm(h, final_norm_g)
```

```python
import functools
import math

import jax
import jax.numpy as jnp
from jax import lax
from jax.experimental import pallas as pl
from jax.experimental.pallas import tpu as pltpu

F32 = jnp.float32
BF16 = jnp.bfloat16
HIGHEST = lax.Precision.HIGHEST

D_MODEL = 1024
CHUNK = 64
SSD_WIDTH = 512
SSD_HEADS = 8
SSD_HEAD_DIM = 64
SSD_GROUPS = 2
SSD_STATE = 128
SSD_CONV = 4
SSD_XBC = 1024
CONF_WIDTH = 512
CONF_CONV = 31
PEER_HEADS = 8
PEER_N_KEYS = 128
PEER_D_HALF = 128
PEER_TOPK = 16
PEER_SLOTS = PEER_HEADS * PEER_TOPK
NORM_EPS = 1e-6

LANES = 128
SUBLANES = 8
ROW_SUBLANES = D_MODEL // 2 // LANES
VMEM_LIMIT = 56 * 1024 * 1024

XBC_TAIL = 8
GLU_TAIL = 32


def _silu(v):
    return v * jax.nn.sigmoid(v)


def _softplus(v):
    return jnp.maximum(v, 0.0) + jnp.log(1.0 + jnp.exp(-jnp.abs(v)))


def _bdot(a, b):
    return jnp.dot(a.astype(BF16), b.astype(BF16), preferred_element_type=F32)


def _mod_kernel(c_ref, w_ref, b_ref, o_ref):
    cond = _silu(c_ref[...])
    o_ref[...] = jnp.dot(cond, w_ref[...], precision=HIGHEST, preferred_element_type=F32) + b_ref[...]


def _mod_call(c, ada_w, ada_b):
    bsz, d = c.shape
    n = ada_w.shape[1]
    return pl.pallas_call(
        _mod_kernel,
        out_shape=jax.ShapeDtypeStruct((bsz, n), F32),
        grid=(n // d,),
        in_specs=[pl.BlockSpec((bsz, d), lambda i: (0, 0)),
                  pl.BlockSpec((d, d), lambda i: (0, i)),
                  pl.BlockSpec((1, d), lambda i: (0, i))],
        out_specs=pl.BlockSpec((bsz, d), lambda i: (0, i)),
        name="mod",
    )(c, ada_w, ada_b.reshape(1, n))


def _inproj_kernel(x_ref, mod_ref, g_ref, wz_ref, wx_ref, wg_ref, wd_ref,
                   z_ref, xbc_ref, glu_ref, dt_ref):
    x = x_ref[...]
    ms = jnp.mean(x * x, axis=-1, keepdims=True)
    y = x * lax.rsqrt(ms + NORM_EPS) * g_ref[...]
    sh = mod_ref[0, 0:1, :]
    sc = mod_ref[0, 1:2, :]
    hn = (y * (1.0 + sc) + sh).astype(BF16)
    z_ref[...] = jnp.dot(hn, wz_ref[...], preferred_element_type=F32)
    xbc_ref[...] = jnp.dot(hn, wx_ref[...], preferred_element_type=F32)
    glu_ref[...] = jnp.dot(hn, wg_ref[...], preferred_element_type=F32)
    dt_ref[...] = jnp.dot(hn, wd_ref[...], preferred_element_type=F32)


def _inproj_call(x2, mod3, norm1_g, wz, wx, wg, wd, seq, tm):
    t, d = x2.shape
    per_b = seq // tm
    const = lambda i: (0, 0)
    row = lambda i: (i, 0)
    return pl.pallas_call(
        _inproj_kernel,
        out_shape=(jax.ShapeDtypeStruct((t, SSD_WIDTH), F32),
                   jax.ShapeDtypeStruct((t, SSD_XBC), F32),
                   jax.ShapeDtypeStruct((t, 2 * CONF_WIDTH), F32),
                   jax.ShapeDtypeStruct((t, LANES), F32)),
        grid=(t // tm,),
        in_specs=[pl.BlockSpec((tm, d), row),
                  pl.BlockSpec((1, 6, d), lambda i: (i // per_b, 0, 0)),
                  pl.BlockSpec((1, d), const),
                  pl.BlockSpec(wz.shape, const),
                  pl.BlockSpec(wx.shape, const),
                  pl.BlockSpec(wg.shape, const),
                  pl.BlockSpec(wd.shape, const)],
        out_specs=(pl.BlockSpec((tm, SSD_WIDTH), row),
                   pl.BlockSpec((tm, SSD_XBC), row),
                   pl.BlockSpec((tm, 2 * CONF_WIDTH), row),
                   pl.BlockSpec((tm, LANES), row)),
        compiler_params=pltpu.CompilerParams(vmem_limit_bytes=VMEM_LIMIT),
        name="inproj",
    )(x2, mod3, norm1_g.reshape(1, d), wz, wx, wg, wd)


def _mixer_kernel(x_ref, z_ref, xbc_ref, glu_ref, dt_ref, mod_ref,
                  cw_ref, cb_ref, dtb_ref, alog_ref, dexp_ref, sng_ref,
                  dww_ref, dwb_ref, lng_ref, lnb_ref, wout_ref, n2g_ref,
                  h1_ref, hn2_ref,
                  xext_ref, gext_ref, hst_ref, xc_ref, xdt_ref, acs_ref, eacs_ref,
                  acst_ref, bmt_ref, y_ref, *, ts):
    nc = ts // CHUNK
    hw = SSD_WIDTH // SSD_GROUPS

    @pl.when(pl.program_id(1) == 0)
    def _():
        xext_ref[0:XBC_TAIL, :] = jnp.zeros((XBC_TAIL, SSD_XBC), F32)
        gext_ref[0:GLU_TAIL, :] = jnp.zeros((GLU_TAIL, CONF_WIDTH), F32)
        hst_ref[...] = jnp.zeros(hst_ref.shape, F32)

    xext_ref[XBC_TAIL:XBC_TAIL + ts, :] = xbc_ref[...]
    acc = cb_ref[...] + cw_ref[0:1, :] * xext_ref[pl.ds(XBC_TAIL - SSD_CONV + 1, ts), :]
    for k in range(1, SSD_CONV):
        acc = acc + cw_ref[k:k + 1, :] * xext_ref[pl.ds(XBC_TAIL - SSD_CONV + 1 + k, ts), :]
    xext_ref[0:XBC_TAIL, :] = xext_ref[ts:ts + XBC_TAIL, :]
    xc_ref[...] = _silu(acc)

    dt = _softplus(dt_ref[...] + dtb_ref[...])
    dta = dt * (-jnp.exp(alog_ref[...]))
    ri = lax.broadcasted_iota(jnp.int32, (ts, ts), 0)
    ci = lax.broadcasted_iota(jnp.int32, (ts, ts), 1)
    ltri = jnp.where(((ri // CHUNK) == (ci // CHUNK)) & (ci <= ri), 1.0, 0.0).astype(F32)
    acs = jnp.dot(ltri, dta, precision=HIGHEST, preferred_element_type=F32)
    acst_ref[...] = acs.T
    er = lax.broadcasted_iota(jnp.int32, (LANES, SSD_WIDTH), 0)
    ec = lax.broadcasted_iota(jnp.int32, (LANES, SSD_WIDTH), 1)
    expand = jnp.where((ec // SSD_HEAD_DIM) == er, 1.0, 0.0).astype(F32)
    dt_exp = jnp.dot(dt, expand, precision=HIGHEST, preferred_element_type=F32)
    acs_exp = jnp.dot(acs, expand, precision=HIGHEST, preferred_element_type=F32)
    acs_ref[...] = acs_exp
    eacs_ref[...] = jnp.exp(acs_exp)
    xdt_ref[...] = xc_ref[:, 0:SSD_WIDTH] * dt_exp
    bmt_ref[...] = xc_ref[:, SSD_WIDTH:SSD_WIDTH + SSD_GROUPS * SSD_STATE].T

    tr = lax.broadcasted_iota(jnp.int32, (CHUNK, CHUNK), 0)
    tc = lax.broadcasted_iota(jnp.int32, (CHUNK, CHUNK), 1)
    tril = tc <= tr
    c_off = SSD_WIDTH + SSD_GROUPS * SSD_STATE

    for c in range(nc):
        r0 = c * CHUNK
        rows = slice(r0, r0 + CHUNK)
        a_last = acs_ref[r0 + CHUNK - 1:r0 + CHUNK, :]
        xw = xdt_ref[rows, :] * jnp.exp(a_last - acs_ref[rows, :])
        cdec = jnp.exp(a_last)
        y_parts = []
        for g in range(SSD_GROUPS):
            cg = xc_ref[rows, c_off + g * SSD_STATE:c_off + (g + 1) * SSD_STATE].astype(BF16)
            bg = xc_ref[rows, SSD_WIDTH + g * SSD_STATE:SSD_WIDTH + (g + 1) * SSD_STATE].astype(BF16)
            cb = lax.dot_general(cg, bg, (((1,), (1,)), ((), ())), preferred_element_type=F32)
            hg = hst_ref[g]
            yoff = jnp.dot(cg, hg.astype(BF16), preferred_element_type=F32)
            st = _bdot(bmt_ref[g * SSD_STATE:(g + 1) * SSD_STATE, rows], xw[:, g * hw:(g + 1) * hw])
            hst_ref[g] = hg * cdec[:, g * hw:(g + 1) * hw] + st
            yds = []
            for hh in range(SSD_HEADS // SSD_GROUPS):
                h = g * (SSD_HEADS // SSD_GROUPS) + hh
                cols = slice(h * SSD_HEAD_DIM, (h + 1) * SSD_HEAD_DIM)
                seg = acs_ref[rows, cols] - acst_ref[h:h + 1, rows]
                dec = jnp.exp(jnp.where(tril, seg, -jnp.inf))
                yds.append(_bdot(cb * dec, xdt_ref[rows, cols]))
            y_parts.append(jnp.concatenate(yds, axis=1) + yoff * eacs_ref[rows, g * hw:(g + 1) * hw])
        y_ref[rows, :] = jnp.concatenate(y_parts, axis=1) + dexp_ref[...] * xc_ref[rows, 0:SSD_WIDTH]

    y = y_ref[...] * _silu(z_ref[...])
    y_ssd = y * lax.rsqrt(jnp.mean(y * y, axis=-1, keepdims=True) + NORM_EPS) * sng_ref[...]

    gext_ref[GLU_TAIL:GLU_TAIL + ts, :] = glu_ref[:, 0:CONF_WIDTH] * jax.nn.sigmoid(glu_ref[:, CONF_WIDTH:])
    base = GLU_TAIL - CONF_CONV + 1
    u = dwb_ref[...] + dww_ref[0:1, :] * gext_ref[pl.ds(base, ts), :]
    for k in range(1, CONF_CONV):
        u = u + dww_ref[k:k + 1, :] * gext_ref[pl.ds(base + k, ts), :]
    gext_ref[0:GLU_TAIL, :] = gext_ref[ts:ts + GLU_TAIL, :]
    mu = jnp.mean(u, axis=-1, keepdims=True)
    uc = u - mu
    var = jnp.mean(uc * uc, axis=-1, keepdims=True)
    y_conf = _silu(uc * lax.rsqrt(var + NORM_EPS) * lng_ref[...] + lnb_ref[...])

    mix = (jnp.dot(y_ssd.astype(BF16), wout_ref[0:SSD_WIDTH, :], preferred_element_type=F32)
           + jnp.dot(y_conf.astype(BF16), wout_ref[SSD_WIDTH:, :], preferred_element_type=F32))
    h1 = x_ref[...] + mod_ref[0, 2:3, :] * mix
    h1_ref[...] = h1
    hn = h1 * lax.rsqrt(jnp.mean(h1 * h1, axis=-1, keepdims=True) + NORM_EPS) * n2g_ref[...]
    hn2_ref[...] = hn * (1.0 + mod_ref[0, 4:5, :]) + mod_ref[0, 3:4, :]


def _mixer_call(x2, z, xbc, glu, dt, mod3, cw, cb, dtb, alog, dexp, sng, dww, dwb, lng, lnb,
                wout, n2g, bsz, seq, ts):
    t, d = x2.shape
    per_b = seq // ts
    row = lambda b, j: (b * per_b + j, 0)
    const = lambda b, j: (0, 0)

    def full(a):
        return pl.BlockSpec(a.shape, const)

    return pl.pallas_call(
        functools.partial(_mixer_kernel, ts=ts),
        out_shape=(jax.ShapeDtypeStruct((t, d), F32), jax.ShapeDtypeStruct((t, d), F32)),
        grid=(bsz, per_b),
        in_specs=[pl.BlockSpec((ts, d), row),
                  pl.BlockSpec((ts, SSD_WIDTH), row),
                  pl.BlockSpec((ts, SSD_XBC), row),
                  pl.BlockSpec((ts, 2 * CONF_WIDTH), row),
                  pl.BlockSpec((ts, LANES), row),
                  pl.BlockSpec((1, 6, d), lambda b, j: (b, 0, 0)),
                  full(cw), full(cb), full(dtb), full(alog), full(dexp), full(sng),
                  full(dww), full(dwb), full(lng), full(lnb), full(wout), full(n2g)],
        out_specs=(pl.BlockSpec((ts, d), row), pl.BlockSpec((ts, d), row)),
        scratch_shapes=[pltpu.VMEM((ts + XBC_TAIL, SSD_XBC), F32),
                        pltpu.VMEM((ts + GLU_TAIL, CONF_WIDTH), F32),
                        pltpu.VMEM((SSD_GROUPS, SSD_STATE, SSD_WIDTH // SSD_GROUPS), F32),
                        pltpu.VMEM((ts, SSD_XBC), F32),
                        pltpu.VMEM((ts, SSD_WIDTH), F32),
                        pltpu.VMEM((ts, SSD_WIDTH), F32),
                        pltpu.VMEM((ts, SSD_WIDTH), F32),
                        pltpu.VMEM((LANES, ts), F32),
                        pltpu.VMEM((SSD_GROUPS * SSD_STATE, ts), F32),
                        pltpu.VMEM((ts, SSD_WIDTH), F32)],
        compiler_params=pltpu.CompilerParams(
            dimension_semantics=("arbitrary", "arbitrary"), vmem_limit_bytes=VMEM_LIMIT),
        name="mixer",
    )(x2, z, xbc, glu, dt, mod3, cw, cb, dtb, alog, dexp, sng, dww, dwb, lng, lnb, wout, n2g)


def _topk_rows(s, k, n):
    iota = lax.broadcasted_iota(jnp.int32, s.shape, 0)
    vals, rows = [], []
    for _ in range(k):
        m = jnp.max(s, axis=0, keepdims=True)
        am = jnp.min(jnp.where(s == m, iota, n), axis=0, keepdims=True)
        vals.append(m)
        rows.append(am)
        s = jnp.where(iota == am, -jnp.inf, s)
    return vals, rows


def _route_kernel(hn_ref, wq_ref, keys_ref, idx_ref, gate_ref):
    q = jnp.dot(hn_ref[...].astype(BF16), wq_ref[...], preferred_element_type=F32)
    tm = q.shape[0]
    nt = (((1,), (1,)), ((), ()))
    for h in range(PEER_HEADS):
        sv, si = [], []
        for i in range(2):
            col = (h * 2 + i) * PEER_D_HALF
            qh = q[:, col:col + PEER_D_HALF].astype(BF16)
            sc = lax.dot_general(keys_ref[h * 2 + i], qh, nt, preferred_element_type=F32)
            v, r = _topk_rows(sc, PEER_TOPK, PEER_N_KEYS)
            sv.append(v)
            si.append(r)
        sv2 = jnp.concatenate(sv[1], axis=0)
        si2 = jnp.concatenate(si[1], axis=0)
        cand = jnp.concatenate([sv[0][a] + sv2 for a in range(PEER_TOPK)], axis=0)
        cidx = jnp.concatenate([si[0][a] * PEER_N_KEYS + si2 for a in range(PEER_TOPK)], axis=0)
        n2 = PEER_TOPK * PEER_TOPK
        iota = lax.broadcasted_iota(jnp.int32, (n2, tm), 0)
        best, eidx = [], []
        for _ in range(PEER_TOPK):
            m = jnp.max(cand, axis=0, keepdims=True)
            p = jnp.min(jnp.where(cand == m, iota, n2), axis=0, keepdims=True)
            hit = iota == p
            best.append(m)
            eidx.append(jnp.max(jnp.where(hit, cidx, -1), axis=0, keepdims=True))
            cand = jnp.where(hit, -jnp.inf, cand)
        best = jnp.concatenate(best, axis=0)
        e = jnp.exp(best - best[0:1, :])
        gate_ref[h * PEER_TOPK:(h + 1) * PEER_TOPK, :] = e / jnp.sum(e, axis=0, keepdims=True)
        idx_ref[h * PEER_TOPK:(h + 1) * PEER_TOPK, :] = jnp.concatenate(eidx, axis=0)


def _route_call(hn2, wq, keys, tm):
    t, d = hn2.shape
    return pl.pallas_call(
        _route_kernel,
        out_shape=(jax.ShapeDtypeStruct((PEER_SLOTS, t), jnp.int32),
                   jax.ShapeDtypeStruct((PEER_SLOTS, t), F32)),
        grid=(t // tm,),
        in_specs=[pl.BlockSpec((tm, d), lambda i: (i, 0)),
                  pl.BlockSpec(wq.shape, lambda i: (0, 0)),
                  pl.BlockSpec(keys.shape, lambda i: (0, 0, 0))],
        out_specs=(pl.BlockSpec((PEER_SLOTS, tm), lambda i: (0, i)),
                   pl.BlockSpec((PEER_SLOTS, tm), lambda i: (0, i))),
        compiler_params=pltpu.CompilerParams(vmem_limit_bytes=VMEM_LIMIT),
        name="route",
    )(hn2, wq, keys)


def _pack_table(tbl):
    n, d = tbl.shape
    bits = lax.bitcast_convert_type(tbl.astype(BF16), jnp.uint16).astype(jnp.uint32)
    word = bits[:, :d // 2] | (bits[:, d // 2:] << 16)
    return word.reshape(n * ROW_SUBLANES, LANES)


def _unpack_lo(w):
    return pltpu.bitcast(w << 16, F32)


def _unpack_hi(w):
    return pltpu.bitcast(w & jnp.uint32(0xFFFF0000), F32)


def _gather_rows(idx_ref, tbl_ref, slot_ref, t):
    for j in range(PEER_SLOTS):
        start = pl.multiple_of(idx_ref[j, t] * ROW_SUBLANES, ROW_SUBLANES)
        slot_ref[j * ROW_SUBLANES:(j + 1) * ROW_SUBLANES, :] = tbl_ref[pl.ds(start, ROW_SUBLANES), :]


def _peer_u_kernel(idx_ref, x_ref, gate_ref, tbl_ref, w_ref, slot_ref, prod_ref, *, tb):
    per_vreg = SUBLANES // ROW_SUBLANES
    lane = lax.broadcasted_iota(jnp.int32, (PEER_SLOTS, LANES), 1)
    for blk in range(tb // LANES):
        def body(tl, acc):
            t = blk * LANES + tl
            _gather_rows(idx_ref, tbl_ref, slot_ref, t)
            xt = x_ref[t]
            xlo = jnp.concatenate([xt[0:ROW_SUBLANES]] * per_vreg, axis=0)
            xhi = jnp.concatenate([xt[ROW_SUBLANES:]] * per_vreg, axis=0)
            words = slot_ref[...].reshape(PEER_SLOTS // per_vreg, SUBLANES, LANES)
            prod = _unpack_lo(words) * xlo[None] + _unpack_hi(words) * xhi[None]
            prod_ref[...] = prod.reshape(PEER_SLOTS * ROW_SUBLANES, LANES)
            part = prod_ref[pl.ds(0, PEER_SLOTS, stride=ROW_SUBLANES), :]
            for r in range(1, ROW_SUBLANES):
                part = part + prod_ref[pl.ds(r, PEER_SLOTS, stride=ROW_SUBLANES), :]
            col = jnp.sum(part, axis=-1, keepdims=True)
            return jnp.where(lane == tl, col, acc)

        act = lax.fori_loop(0, LANES, body, jnp.zeros((PEER_SLOTS, LANES), F32))
        gelu = 0.5 * act * (1.0 + lax.erf(act * (1.0 / math.sqrt(2.0))))
        w_ref[:, blk * LANES:(blk + 1) * LANES] = gate_ref[:, blk * LANES:(blk + 1) * LANES] * gelu


def _peer_v_kernel(idx_ref, w_ref, tbl_ref, o_ref, slot_ref, *, tb):
    per_vreg = SUBLANES // ROW_SUBLANES
    n_vreg = PEER_SLOTS // per_vreg
    sub = lax.broadcasted_iota(jnp.int32, (SUBLANES, LANES), 0)

    def body(t, carry):
        _gather_rows(idx_ref, tbl_ref, slot_ref, t)
        acc_lo = jnp.zeros((SUBLANES, LANES), F32)
        acc_hi = jnp.zeros((SUBLANES, LANES), F32)
        for k in range(n_vreg):
            words = slot_ref[k * SUBLANES:(k + 1) * SUBLANES, :]
            wv = jnp.full((SUBLANES, LANES), w_ref[per_vreg * k, t], F32)
            for r in range(1, per_vreg):
                wv = jnp.where(sub >= r * ROW_SUBLANES, w_ref[per_vreg * k + r, t], wv)
            acc_lo = acc_lo + _unpack_lo(words) * wv
            acc_hi = acc_hi + _unpack_hi(words) * wv
        lo = acc_lo[0:ROW_SUBLANES]
        hi = acc_hi[0:ROW_SUBLANES]
        for r in range(1, per_vreg):
            lo = lo + acc_lo[r * ROW_SUBLANES:(r + 1) * ROW_SUBLANES]
            hi = hi + acc_hi[r * ROW_SUBLANES:(r + 1) * ROW_SUBLANES]
        o_ref[t] = jnp.concatenate([lo, hi], axis=0)
        return carry

    lax.fori_loop(0, tb, body, 0)


def _table_spec(tbl):
    return pl.BlockSpec(tbl.shape, lambda i: (0, 0), pipeline_mode=pl.Buffered(1))


def _peer_u_call(idx_t, x3, gate_t, tbl, tb):
    t = x3.shape[0]
    return pl.pallas_call(
        functools.partial(_peer_u_kernel, tb=tb),
        out_shape=jax.ShapeDtypeStruct((PEER_SLOTS, t), F32),
        grid=(t // tb,),
        in_specs=[pl.BlockSpec((PEER_SLOTS, tb), lambda i: (0, i), memory_space=pltpu.SMEM),
                  pl.BlockSpec((tb, SUBLANES, LANES), lambda i: (i, 0, 0)),
                  pl.BlockSpec((PEER_SLOTS, tb), lambda i: (0, i)),
                  _table_spec(tbl)],
        out_specs=pl.BlockSpec((PEER_SLOTS, tb), lambda i: (0, i)),
        scratch_shapes=[pltpu.VMEM((PEER_SLOTS * ROW_SUBLANES, LANES), jnp.uint32),
                        pltpu.VMEM((PEER_SLOTS * ROW_SUBLANES, LANES), F32)],
        compiler_params=pltpu.CompilerParams(vmem_limit_bytes=VMEM_LIMIT),
        name="peer_u",
    )(idx_t, x3, gate_t, tbl)


def _peer_v_call(idx_t, w_t, tbl, tb):
    t = idx_t.shape[1]
    return pl.pallas_call(
        functools.partial(_peer_v_kernel, tb=tb),
        out_shape=jax.ShapeDtypeStruct((t, SUBLANES, LANES), F32),
        grid=(t // tb,),
        in_specs=[pl.BlockSpec((PEER_SLOTS, tb), lambda i: (0, i), memory_space=pltpu.SMEM),
                  pl.BlockSpec((PEER_SLOTS, tb), lambda i: (0, i), memory_space=pltpu.SMEM),
                  _table_spec(tbl)],
        out_specs=pl.BlockSpec((tb, SUBLANES, LANES), lambda i: (i, 0, 0)),
        scratch_shapes=[pltpu.VMEM((PEER_SLOTS * ROW_SUBLANES, LANES), jnp.uint32)],
        compiler_params=pltpu.CompilerParams(vmem_limit_bytes=VMEM_LIMIT),
        name="peer_v",
    )(idx_t, w_t, tbl)


def _final_kernel(h1_ref, p_ref, mod_ref, g_ref, o_ref):
    h = h1_ref[...] + mod_ref[0, 5:6, :] * p_ref[...]
    o_ref[...] = h * lax.rsqrt(jnp.mean(h * h, axis=-1, keepdims=True) + NORM_EPS) * g_ref[...]


def _final_call(h1, peer, mod3, g, seq, tm):
    t, d = h1.shape
    per_b = seq // tm
    row = lambda i: (i, 0)
    return pl.pallas_call(
        _final_kernel,
        out_shape=jax.ShapeDtypeStruct((t, d), F32),
        grid=(t // tm,),
        in_specs=[pl.BlockSpec((tm, d), row), pl.BlockSpec((tm, d), row),
                  pl.BlockSpec((1, 6, d), lambda i: (i // per_b, 0, 0)),
                  pl.BlockSpec((1, d), lambda i: (0, 0))],
        out_specs=pl.BlockSpec((tm, d), row),
        name="final",
    )(h1, peer, mod3, g.reshape(1, d))


def _pad_lanes(v):
    return jnp.pad(v.reshape(1, -1), ((0, 0), (0, LANES - v.shape[-1])))


def kernel(x, c, ada_w, ada_b, norm1_g, w_in, ssd_conv_w, ssd_conv_b, ssd_dt_bias, ssd_a_log, ssd_d, ssd_norm_g, conf_dw_w, conf_dw_b, conf_ln_g, conf_ln_b, w_out, norm2_g, peer_w_query, peer_sub_keys, peer_u, peer_v, final_norm_g):
    bsz, seq, d = x.shape
    assert d == D_MODEL and ada_w.shape[0] == 1
    t = bsz * seq
    tm = min(512, seq)
    ts = min(256, seq)
    tb = min(256, seq)
    x2 = x.reshape(t, d)

    mod3 = _mod_call(c, ada_w[0], ada_b[0]).reshape(bsz, 6, d)

    wi = w_in[0]
    o1 = SSD_WIDTH
    o2 = o1 + SSD_XBC
    o3 = o2 + SSD_HEADS
    wz = wi[:, :o1].astype(BF16)
    wx = wi[:, o1:o2].astype(BF16)
    wd = jnp.pad(wi[:, o2:o3], ((0, 0), (0, LANES - SSD_HEADS))).astype(BF16)
    wg = wi[:, o3:].astype(BF16)
    z, xbc, glu, dt = _inproj_call(x2, mod3, norm1_g[0], wz, wx, wg, wd, seq, tm)

    h1, hn2 = _mixer_call(
        x2, z, xbc, glu, dt, mod3,
        ssd_conv_w[0], ssd_conv_b[0].reshape(1, -1), _pad_lanes(ssd_dt_bias[0]), _pad_lanes(ssd_a_log[0]),
        jnp.repeat(ssd_d[0], SSD_HEAD_DIM).reshape(1, -1), ssd_norm_g[0].reshape(1, -1),
        conf_dw_w[0], conf_dw_b[0].reshape(1, -1), conf_ln_g[0].reshape(1, -1), conf_ln_b[0].reshape(1, -1),
        w_out[0].astype(BF16), norm2_g[0].reshape(1, -1), bsz, seq, ts)

    keys = peer_sub_keys[0].reshape(PEER_HEADS * 2, PEER_N_KEYS, PEER_D_HALF).astype(BF16)
    idx_t, gate_t = _route_call(hn2, peer_w_query[0].astype(BF16), keys, tb)

    w_t = _peer_u_call(idx_t, hn2.reshape(t, SUBLANES, LANES), gate_t, _pack_table(peer_u[0]), tb)
    peer = _peer_v_call(idx_t, w_t, _pack_table(peer_v[0]), tb)

    out = _final_call(h1, peer.reshape(t, d), mod3, final_norm_g, seq, tm)
    return out.reshape(bsz, seq, d)
```

```python
import functools
import math

import jax
import jax.numpy as jnp
from jax import lax
from jax.experimental import pallas as pl
from jax.experimental.pallas import tpu as pltpu

F32 = jnp.float32
BF16 = jnp.bfloat16
HIGHEST = lax.Precision.HIGHEST

D_MODEL = 1024
CHUNK = 64
SSD_WIDTH = 512
SSD_HEADS = 8
SSD_HEAD_DIM = 64
SSD_GROUPS = 2
SSD_STATE = 128
SSD_CONV = 4
SSD_XBC = 1024
CONF_WIDTH = 512
CONF_CONV = 31
PEER_HEADS = 8
PEER_N_KEYS = 128
PEER_D_HALF = 128
PEER_TOPK = 16
PEER_SLOTS = PEER_HEADS * PEER_TOPK
NORM_EPS = 1e-6

LANES = 128
SUBLANES = 8
ROW_SUBLANES = D_MODEL // 2 // LANES
VMEM_LIMIT = 56 * 1024 * 1024

XBC_TAIL = 8
GLU_TAIL = 32


def _silu(v):
    return v * jax.nn.sigmoid(v)


def _softplus(v):
    return jnp.maximum(v, 0.0) + jnp.log(1.0 + jnp.exp(-jnp.abs(v)))


def _bdot(a, b):
    return jnp.dot(a.astype(BF16), b.astype(BF16), preferred_element_type=F32)


def _mod_kernel(c_ref, w_ref, b_ref, o_ref):
    cond = _silu(c_ref[...])
    o_ref[...] = jnp.dot(cond, w_ref[...], precision=HIGHEST, preferred_element_type=F32) + b_ref[...]


def _mod_call(c, ada_w, ada_b):
    bsz, d = c.shape
    n = ada_w.shape[1]
    return pl.pallas_call(
        _mod_kernel,
        out_shape=jax.ShapeDtypeStruct((bsz, n), F32),
        grid=(n // d,),
        in_specs=[pl.BlockSpec((bsz, d), lambda i: (0, 0)),
                  pl.BlockSpec((d, d), lambda i: (0, i)),
                  pl.BlockSpec((1, d), lambda i: (0, i))],
        out_specs=pl.BlockSpec((bsz, d), lambda i: (0, i)),
        name="mod",
    )(c, ada_w, ada_b.reshape(1, n))


def _inproj_kernel(x_ref, mod_ref, g_ref, wz_ref, wx_ref, wg_ref, wd_ref,
                   z_ref, xbc_ref, glu_ref, dt_ref):
    x = x_ref[...]
    ms = jnp.mean(x * x, axis=-1, keepdims=True)
    y = x * lax.rsqrt(ms + NORM_EPS) * g_ref[...]
    sh = mod_ref[0, 0:1, :]
    sc = mod_ref[0, 1:2, :]
    hn = (y * (1.0 + sc) + sh).astype(BF16)
    z_ref[...] = jnp.dot(hn, wz_ref[...], preferred_element_type=F32)
    xbc_ref[...] = jnp.dot(hn, wx_ref[...], preferred_element_type=F32)
    glu_ref[...] = jnp.dot(hn, wg_ref[...], preferred_element_type=F32)
    dt_ref[...] = jnp.dot(hn, wd_ref[...], preferred_element_type=F32)


def _inproj_call(x2, mod3, norm1_g, wz, wx, wg, wd, seq, tm):
    t, d = x2.shape
    per_b = seq // tm
    const = lambda i: (0, 0)
    row = lambda i: (i, 0)
    return pl.pallas_call(
        _inproj_kernel,
        out_shape=(jax.ShapeDtypeStruct((t, SSD_WIDTH), F32),
                   jax.ShapeDtypeStruct((t, SSD_XBC), F32),
                   jax.ShapeDtypeStruct((t, 2 * CONF_WIDTH), F32),
                   jax.ShapeDtypeStruct((t, LANES), F32)),
        grid=(t // tm,),
        in_specs=[pl.BlockSpec((tm, d), row),
                  pl.BlockSpec((1, 6, d), lambda i: (i // per_b, 0, 0)),
                  pl.BlockSpec((1, d), const),
                  pl.BlockSpec(wz.shape, const),
                  pl.BlockSpec(wx.shape, const),
                  pl.BlockSpec(wg.shape, const),
                  pl.BlockSpec(wd.shape, const)],
        out_specs=(pl.BlockSpec((tm, SSD_WIDTH), row),
                   pl.BlockSpec((tm, SSD_XBC), row),
                   pl.BlockSpec((tm, 2 * CONF_WIDTH), row),
                   pl.BlockSpec((tm, LANES), row)),
        compiler_params=pltpu.CompilerParams(vmem_limit_bytes=VMEM_LIMIT),
        name="inproj",
    )(x2, mod3, norm1_g.reshape(1, d), wz, wx, wg, wd)


def _mixer_kernel(x_ref, z_ref, xbc_ref, glu_ref, dt_ref, mod_ref,
                  cw_ref, cb_ref, dtb_ref, alog_ref, dexp_ref, sng_ref,
                  dww_ref, dwb_ref, lng_ref, lnb_ref, wout_ref, n2g_ref,
                  h1_ref, hn2_ref,
                  xext_ref, gext_ref, hst_ref, xc_ref, xdt_ref, acs_ref, eacs_ref,
                  acst_ref, bmt_ref, y_ref, *, ts):
    nc = ts // CHUNK
    hw = SSD_WIDTH // SSD_GROUPS

    @pl.when(pl.program_id(1) == 0)
    def _():
        xext_ref[0:XBC_TAIL, :] = jnp.zeros((XBC_TAIL, SSD_XBC), F32)
        gext_ref[0:GLU_TAIL, :] = jnp.zeros((GLU_TAIL, CONF_WIDTH), F32)
        hst_ref[...] = jnp.zeros(hst_ref.shape, F32)

    xext_ref[XBC_TAIL:XBC_TAIL + ts, :] = xbc_ref[...]
    acc = cb_ref[...] + cw_ref[0:1, :] * xext_ref[pl.ds(XBC_TAIL - SSD_CONV + 1, ts), :]
    for k in range(1, SSD_CONV):
        acc = acc + cw_ref[k:k + 1, :] * xext_ref[pl.ds(XBC_TAIL - SSD_CONV + 1 + k, ts), :]
    xext_ref[0:XBC_TAIL, :] = xext_ref[ts:ts + XBC_TAIL, :]
    xc_ref[...] = _silu(acc)

    dt = _softplus(dt_ref[...] + dtb_ref[...])
    dta = dt * (-jnp.exp(alog_ref[...]))
    ri = lax.broadcasted_iota(jnp.int32, (ts, ts), 0)
    ci = lax.broadcasted_iota(jnp.int32, (ts, ts), 1)
    ltri = jnp.where(((ri // CHUNK) == (ci // CHUNK)) & (ci <= ri), 1.0, 0.0).astype(F32)
    acs = jnp.dot(ltri, dta, precision=HIGHEST, preferred_element_type=F32)
    acst_ref[...] = acs.T
    er = lax.broadcasted_iota(jnp.int32, (LANES, SSD_WIDTH), 0)
    ec = lax.broadcasted_iota(jnp.int32, (LANES, SSD_WIDTH), 1)
    expand = jnp.where((ec // SSD_HEAD_DIM) == er, 1.0, 0.0).astype(F32)
    dt_exp = jnp.dot(dt, expand, precision=HIGHEST, preferred_element_type=F32)
    acs_exp = jnp.dot(acs, expand, precision=HIGHEST, preferred_element_type=F32)
    acs_ref[...] = acs_exp
    eacs_ref[...] = jnp.exp(acs_exp)
    xdt_ref[...] = xc_ref[:, 0:SSD_WIDTH] * dt_exp
    bmt_ref[...] = xc_ref[:, SSD_WIDTH:SSD_WIDTH + SSD_GROUPS * SSD_STATE].T

    tr = lax.broadcasted_iota(jnp.int32, (CHUNK, CHUNK), 0)
    tc = lax.broadcasted_iota(jnp.int32, (CHUNK, CHUNK), 1)
    tril = tc <= tr
    c_off = SSD_WIDTH + SSD_GROUPS * SSD_STATE

    for c in range(nc):
        r0 = c * CHUNK
        rows = slice(r0, r0 + CHUNK)
        a_last = acs_ref[r0 + CHUNK - 1:r0 + CHUNK, :]
        xw = xdt_ref[rows, :] * jnp.exp(a_last - acs_ref[rows, :])
        cdec = jnp.exp(a_last)
        y_parts = []
        for g in range(SSD_GROUPS):
            cg = xc_ref[rows, c_off + g * SSD_STATE:c_off + (g + 1) * SSD_STATE].astype(BF16)
            bg = xc_ref[rows, SSD_WIDTH + g * SSD_STATE:SSD_WIDTH + (g + 1) * SSD_STATE].astype(BF16)
            cb = lax.dot_general(cg, bg, (((1,), (1,)), ((), ())), preferred_element_type=F32)
            hg = hst_ref[g]
            yoff = jnp.dot(cg, hg.astype(BF16), preferred_element_type=F32)
            st = _bdot(bmt_ref[g * SSD_STATE:(g + 1) * SSD_STATE, rows], xw[:, g * hw:(g + 1) * hw])
            hst_ref[g] = hg * cdec[:, g * hw:(g + 1) * hw] + st
            yds = []
            for hh in range(SSD_HEADS // SSD_GROUPS):
                h = g * (SSD_HEADS // SSD_GROUPS) + hh
                cols = slice(h * SSD_HEAD_DIM, (h + 1) * SSD_HEAD_DIM)
                seg = acs_ref[rows, cols] - acst_ref[h:h + 1, rows]
                dec = jnp.exp(jnp.where(tril, seg, -jnp.inf))
                yds.append(_bdot(cb * dec, xdt_ref[rows, cols]))
            y_parts.append(jnp.concatenate(yds, axis=1) + yoff * eacs_ref[rows, g * hw:(g + 1) * hw])
        y_ref[rows, :] = jnp.concatenate(y_parts, axis=1) + dexp_ref[...] * xc_ref[rows, 0:SSD_WIDTH]

    y = y_ref[...] * _silu(z_ref[...])
    y_ssd = y * lax.rsqrt(jnp.mean(y * y, axis=-1, keepdims=True) + NORM_EPS) * sng_ref[...]

    gext_ref[GLU_TAIL:GLU_TAIL + ts, :] = glu_ref[:, 0:CONF_WIDTH] * jax.nn.sigmoid(glu_ref[:, CONF_WIDTH:])
    base = GLU_TAIL - CONF_CONV + 1
    u = dwb_ref[...] + dww_ref[0:1, :] * gext_ref[pl.ds(base, ts), :]
    for k in range(1, CONF_CONV):
        u = u + dww_ref[k:k + 1, :] * gext_ref[pl.ds(base + k, ts), :]
    gext_ref[0:GLU_TAIL, :] = gext_ref[ts:ts + GLU_TAIL, :]
    mu = jnp.mean(u, axis=-1, keepdims=True)
    uc = u - mu
    var = jnp.mean(uc * uc, axis=-1, keepdims=True)
    y_conf = _silu(uc * lax.rsqrt(var + NORM_EPS) * lng_ref[...] + lnb_ref[...])

    mix = (jnp.dot(y_ssd.astype(BF16), wout_ref[0:SSD_WIDTH, :], preferred_element_type=F32)
           + jnp.dot(y_conf.astype(BF16), wout_ref[SSD_WIDTH:, :], preferred_element_type=F32))
    h1 = x_ref[...] + mod_ref[0, 2:3, :] * mix
    h1_ref[...] = h1
    hn = h1 * lax.rsqrt(jnp.mean(h1 * h1, axis=-1, keepdims=True) + NORM_EPS) * n2g_ref[...]
    hn2_ref[...] = hn * (1.0 + mod_ref[0, 4:5, :]) + mod_ref[0, 3:4, :]


def _mixer_call(x2, z, xbc, glu, dt, mod3, cw, cb, dtb, alog, dexp, sng, dww, dwb, lng, lnb,
                wout, n2g, bsz, seq, ts):
    t, d = x2.shape
    per_b = seq // ts
    row = lambda b, j: (b * per_b + j, 0)
    const = lambda b, j: (0, 0)

    def full(a):
        return pl.BlockSpec(a.shape, const)

    return pl.pallas_call(
        functools.partial(_mixer_kernel, ts=ts),
        out_shape=(jax.ShapeDtypeStruct((t, d), F32), jax.ShapeDtypeStruct((t, d), F32)),
        grid=(bsz, per_b),
        in_specs=[pl.BlockSpec((ts, d), row),
                  pl.BlockSpec((ts, SSD_WIDTH), row),
                  pl.BlockSpec((ts, SSD_XBC), row),
                  pl.BlockSpec((ts, 2 * CONF_WIDTH), row),
                  pl.BlockSpec((ts, LANES), row),
                  pl.BlockSpec((1, 6, d), lambda b, j: (b, 0, 0)),
                  full(cw), full(cb), full(dtb), full(alog), full(dexp), full(sng),
                  full(dww), full(dwb), full(lng), full(lnb), full(wout), full(n2g)],
        out_specs=(pl.BlockSpec((ts, d), row), pl.BlockSpec((ts, d), row)),
        scratch_shapes=[pltpu.VMEM((ts + XBC_TAIL, SSD_XBC), F32),
                        pltpu.VMEM((ts + GLU_TAIL, CONF_WIDTH), F32),
                        pltpu.VMEM((SSD_GROUPS, SSD_STATE, SSD_WIDTH // SSD_GROUPS), F32),
                        pltpu.VMEM((ts, SSD_XBC), F32),
                        pltpu.VMEM((ts, SSD_WIDTH), F32),
                        pltpu.VMEM((ts, SSD_WIDTH), F32),
                        pltpu.VMEM((ts, SSD_WIDTH), F32),
                        pltpu.VMEM((LANES, ts), F32),
                        pltpu.VMEM((SSD_GROUPS * SSD_STATE, ts), F32),
                        pltpu.VMEM((ts, SSD_WIDTH), F32)],
        compiler_params=pltpu.CompilerParams(
            dimension_semantics=("arbitrary", "arbitrary"), vmem_limit_bytes=VMEM_LIMIT),
        name="mixer",
    )(x2, z, xbc, glu, dt, mod3, cw, cb, dtb, alog, dexp, sng, dww, dwb, lng, lnb, wout, n2g)


_PAIR_COUNTS = tuple(PEER_TOPK // (a + 1) for a in range(PEER_TOPK))


def _topk_rows(s, k):
    n = s.shape[0]
    iota = lax.broadcasted_iota(jnp.int32, s.shape, 0).astype(F32)
    vals, rows = [], []
    for _ in range(k):
        m = jnp.max(s, axis=0, keepdims=True)
        am = jnp.min(jnp.where(s == m, iota, float(n)), axis=0, keepdims=True)
        vals.append(m)
        rows.append(am)
        s = jnp.where(iota == am, -jnp.inf, s)
    return vals, rows


def _route_kernel(hn_ref, wq_ref, keys_ref, idx_ref, gate_ref):
    q = jnp.dot(hn_ref[...].astype(BF16), wq_ref[...], preferred_element_type=F32)
    tm = q.shape[0]
    nt = (((1,), (1,)), ((), ()))
    n_cand = sum(_PAIR_COUNTS)
    n_pad = -n_cand % SUBLANES
    rows_out = []
    for h in range(PEER_HEADS):
        sv, si = [], []
        for i in range(2):
            col = (h * 2 + i) * PEER_D_HALF
            qh = q[:, col:col + PEER_D_HALF].astype(BF16)
            sc = lax.dot_general(keys_ref[h * 2 + i], qh, nt, preferred_element_type=F32)
            v, r = _topk_rows(sc, PEER_TOPK)
            sv.append(v)
            si.append(r)
        sv2 = jnp.concatenate(sv[1], axis=0)
        si2 = jnp.concatenate(si[1], axis=0)
        cand = jnp.concatenate([sv[0][a] + sv2[0:nb] for a, nb in enumerate(_PAIR_COUNTS)]
                               + [jnp.full((n_pad, tm), -jnp.inf, F32)], axis=0)
        cidx = jnp.concatenate([si[0][a] * float(PEER_N_KEYS) + si2[0:nb] for a, nb in enumerate(_PAIR_COUNTS)]
                               + [jnp.zeros((n_pad, tm), F32)], axis=0)
        iota = lax.broadcasted_iota(jnp.int32, cand.shape, 0).astype(F32)
        best, eidx = [], []
        for _ in range(PEER_TOPK):
            m = jnp.max(cand, axis=0, keepdims=True)
            p = jnp.min(jnp.where(cand == m, iota, float(cand.shape[0])), axis=0, keepdims=True)
            hit = iota == p
            best.append(m)
            eidx.append(jnp.max(jnp.where(hit, cidx, -1.0), axis=0, keepdims=True))
            cand = jnp.where(hit, -jnp.inf, cand)
        best = jnp.concatenate(best, axis=0)
        e = jnp.exp(best - best[0:1, :])
        gate_ref[h * PEER_TOPK:(h + 1) * PEER_TOPK, :] = e / jnp.sum(e, axis=0, keepdims=True)
        rows_out.append(jnp.concatenate(eidx, axis=0))
    expert = jnp.concatenate(rows_out, axis=0)
    idx_ref[...] = (expert.T * float(ROW_SUBLANES)).astype(jnp.int32)


def _route_call(hn2, wq, keys, tm):
    t, d = hn2.shape
    return pl.pallas_call(
        _route_kernel,
        out_shape=(jax.ShapeDtypeStruct((t, PEER_SLOTS), jnp.int32),
                   jax.ShapeDtypeStruct((PEER_SLOTS, t), F32)),
        grid=(t // tm,),
        in_specs=[pl.BlockSpec((tm, d), lambda i: (i, 0)),
                  pl.BlockSpec(wq.shape, lambda i: (0, 0)),
                  pl.BlockSpec(keys.shape, lambda i: (0, 0, 0))],
        out_specs=(pl.BlockSpec((tm, PEER_SLOTS), lambda i: (i, 0)),
                   pl.BlockSpec((PEER_SLOTS, tm), lambda i: (0, i))),
        compiler_params=pltpu.CompilerParams(vmem_limit_bytes=VMEM_LIMIT),
        name="route",
    )(hn2, wq, keys)


def _pack_table(tbl):
    n, d = tbl.shape
    bits = lax.bitcast_convert_type(tbl.astype(BF16), jnp.uint16).astype(jnp.uint32)
    word = bits[:, :d // 2] | (bits[:, d // 2:] << 16)
    return word.reshape(n * ROW_SUBLANES, LANES)


def _unpack_lo(w):
    return pltpu.bitcast(w << 16, F32)


def _unpack_hi(w):
    return pltpu.bitcast(w & jnp.uint32(0xFFFF0000), F32)


def _gather_rows(idx_ref, tbl_ref, slot_ref, t):
    for j in range(PEER_SLOTS):
        start = pl.multiple_of(idx_ref[t, j], ROW_SUBLANES)
        slot_ref[j * ROW_SUBLANES:(j + 1) * ROW_SUBLANES, :] = tbl_ref[pl.ds(start, ROW_SUBLANES), :]


def _peer_u_kernel(idx_ref, x_ref, gate_ref, tbl_ref, w_ref, slot_a, slot_b, prod_a, prod_b, *, tb):
    per_vreg = SUBLANES // ROW_SUBLANES
    lane = lax.broadcasted_iota(jnp.int32, (PEER_SLOTS, LANES), 1)

    def products(slot_ref, prod_ref, t):
        xt = x_ref[t]
        xlo = jnp.concatenate([xt[0:ROW_SUBLANES]] * per_vreg, axis=0)
        xhi = jnp.concatenate([xt[ROW_SUBLANES:]] * per_vreg, axis=0)
        words = slot_ref[...].reshape(PEER_SLOTS // per_vreg, SUBLANES, LANES)
        prod = _unpack_lo(words) * xlo[None] + _unpack_hi(words) * xhi[None]
        prod_ref[...] = prod.reshape(PEER_SLOTS * ROW_SUBLANES, LANES)

    def reduce_into(prod_ref, tl, acc):
        part = prod_ref[pl.ds(0, PEER_SLOTS, stride=ROW_SUBLANES), :]
        for r in range(1, ROW_SUBLANES):
            part = part + prod_ref[pl.ds(r, PEER_SLOTS, stride=ROW_SUBLANES), :]
        col = jnp.sum(part, axis=-1, keepdims=True)
        return jnp.where(lane == tl, col, acc)

    prod_b[...] = jnp.zeros(prod_b.shape, F32)
    for blk in range(tb // LANES):
        base = blk * LANES
        _gather_rows(idx_ref, tbl_ref, slot_a, base)

        def body(i, acc):
            t0 = base + 2 * i
            acc = reduce_into(prod_b, 2 * i - 1, acc)
            products(slot_a, prod_a, t0)
            _gather_rows(idx_ref, tbl_ref, slot_b, t0 + 1)
            acc = reduce_into(prod_a, 2 * i, acc)
            products(slot_b, prod_b, t0 + 1)
            _gather_rows(idx_ref, tbl_ref, slot_a, jnp.minimum(t0 + 2, tb - 1))
            return acc

        act = lax.fori_loop(0, LANES // 2, body, jnp.zeros((PEER_SLOTS, LANES), F32))
        act = reduce_into(prod_b, LANES - 1, act)
        gelu = 0.5 * act * (1.0 + lax.erf(act * (1.0 / math.sqrt(2.0))))
        w_ref[blk * LANES:(blk + 1) * LANES, :] = (gate_ref[:, blk * LANES:(blk + 1) * LANES] * gelu).T


def _peer_v_kernel(idx_ref, w_ref, tbl_ref, o_ref, slot_ref, *, tb):
    per_vreg = SUBLANES // ROW_SUBLANES
    n_vreg = PEER_SLOTS // per_vreg
    sub = lax.broadcasted_iota(jnp.int32, (SUBLANES, LANES), 0)

    def body(t, carry):
        _gather_rows(idx_ref, tbl_ref, slot_ref, t)
        acc_lo = jnp.zeros((SUBLANES, LANES), F32)
        acc_hi = jnp.zeros((SUBLANES, LANES), F32)
        for k in range(n_vreg):
            words = slot_ref[k * SUBLANES:(k + 1) * SUBLANES, :]
            wv = jnp.full((SUBLANES, LANES), w_ref[t, per_vreg * k], F32)
            for r in range(1, per_vreg):
                wv = jnp.where(sub >= r * ROW_SUBLANES, w_ref[t, per_vreg * k + r], wv)
            acc_lo = acc_lo + _unpack_lo(words) * wv
            acc_hi = acc_hi + _unpack_hi(words) * wv
        lo = acc_lo[0:ROW_SUBLANES]
        hi = acc_hi[0:ROW_SUBLANES]
        for r in range(1, per_vreg):
            lo = lo + acc_lo[r * ROW_SUBLANES:(r + 1) * ROW_SUBLANES]
            hi = hi + acc_hi[r * ROW_SUBLANES:(r + 1) * ROW_SUBLANES]
        o_ref[t] = jnp.concatenate([lo, hi], axis=0)
        return carry

    lax.fori_loop(0, tb, body, 0)


def _table_spec(tbl):
    return pl.BlockSpec(tbl.shape, lambda i: (0, 0), pipeline_mode=pl.Buffered(1))


def _peer_u_call(idx_t, x3, gate_t, tbl, tb):
    t = x3.shape[0]
    return pl.pallas_call(
        functools.partial(_peer_u_kernel, tb=tb),
        out_shape=jax.ShapeDtypeStruct((t, PEER_SLOTS), F32),
        grid=(t // tb,),
        in_specs=[pl.BlockSpec((tb, PEER_SLOTS), lambda i: (i, 0), memory_space=pltpu.SMEM),
                  pl.BlockSpec((tb, SUBLANES, LANES), lambda i: (i, 0, 0)),
                  pl.BlockSpec((PEER_SLOTS, tb), lambda i: (0, i)),
                  _table_spec(tbl)],
        out_specs=pl.BlockSpec((tb, PEER_SLOTS), lambda i: (i, 0)),
        scratch_shapes=[pltpu.VMEM((PEER_SLOTS * ROW_SUBLANES, LANES), jnp.uint32),
                        pltpu.VMEM((PEER_SLOTS * ROW_SUBLANES, LANES), jnp.uint32),
                        pltpu.VMEM((PEER_SLOTS * ROW_SUBLANES, LANES), F32),
                        pltpu.VMEM((PEER_SLOTS * ROW_SUBLANES, LANES), F32)],
        compiler_params=pltpu.CompilerParams(vmem_limit_bytes=VMEM_LIMIT),
        name="peer_u",
    )(idx_t, x3, gate_t, tbl)


def _peer_v_call(idx_t, w_t, tbl, tb):
    t = idx_t.shape[0]
    return pl.pallas_call(
        functools.partial(_peer_v_kernel, tb=tb),
        out_shape=jax.ShapeDtypeStruct((t, SUBLANES, LANES), F32),
        grid=(t // tb,),
        in_specs=[pl.BlockSpec((tb, PEER_SLOTS), lambda i: (i, 0), memory_space=pltpu.SMEM),
                  pl.BlockSpec((tb, PEER_SLOTS), lambda i: (i, 0), memory_space=pltpu.SMEM),
                  _table_spec(tbl)],
        out_specs=pl.BlockSpec((tb, SUBLANES, LANES), lambda i: (i, 0, 0)),
        scratch_shapes=[pltpu.VMEM((PEER_SLOTS * ROW_SUBLANES, LANES), jnp.uint32)],
        compiler_params=pltpu.CompilerParams(vmem_limit_bytes=VMEM_LIMIT),
        name="peer_v",
    )(idx_t, w_t, tbl)


def _final_kernel(h1_ref, p_ref, mod_ref, g_ref, o_ref):
    h = h1_ref[...] + mod_ref[0, 5:6, :] * p_ref[...]
    o_ref[...] = h * lax.rsqrt(jnp.mean(h * h, axis=-1, keepdims=True) + NORM_EPS) * g_ref[...]


def _final_call(h1, peer, mod3, g, seq, tm):
    t, d = h1.shape
    per_b = seq // tm
    row = lambda i: (i, 0)
    return pl.pallas_call(
        _final_kernel,
        out_shape=jax.ShapeDtypeStruct((t, d), F32),
        grid=(t // tm,),
        in_specs=[pl.BlockSpec((tm, d), row), pl.BlockSpec((tm, d), row),
                  pl.BlockSpec((1, 6, d), lambda i: (i // per_b, 0, 0)),
                  pl.BlockSpec((1, d), lambda i: (0, 0))],
        out_specs=pl.BlockSpec((tm, d), row),
        name="final",
    )(h1, peer, mod3, g.reshape(1, d))


def _pad_lanes(v):
    return jnp.pad(v.reshape(1, -1), ((0, 0), (0, LANES - v.shape[-1])))


def kernel(x, c, ada_w, ada_b, norm1_g, w_in, ssd_conv_w, ssd_conv_b, ssd_dt_bias, ssd_a_log, ssd_d, ssd_norm_g, conf_dw_w, conf_dw_b, conf_ln_g, conf_ln_b, w_out, norm2_g, peer_w_query, peer_sub_keys, peer_u, peer_v, final_norm_g):
    bsz, seq, d = x.shape
    assert d == D_MODEL and ada_w.shape[0] == 1
    t = bsz * seq
    tm = min(512, seq)
    ts = min(256, seq)
    tb = min(256, seq)
    x2 = x.reshape(t, d)

    mod3 = _mod_call(c, ada_w[0], ada_b[0]).reshape(bsz, 6, d)

    wi = w_in[0]
    o1 = SSD_WIDTH
    o2 = o1 + SSD_XBC
    o3 = o2 + SSD_HEADS
    wz = wi[:, :o1].astype(BF16)
    wx = wi[:, o1:o2].astype(BF16)
    wd = jnp.pad(wi[:, o2:o3], ((0, 0), (0, LANES - SSD_HEADS))).astype(BF16)
    wg = wi[:, o3:].astype(BF16)
    z, xbc, glu, dt = _inproj_call(x2, mod3, norm1_g[0], wz, wx, wg, wd, seq, tm)

    h1, hn2 = _mixer_call(
        x2, z, xbc, glu, dt, mod3,
        ssd_conv_w[0], ssd_conv_b[0].reshape(1, -1), _pad_lanes(ssd_dt_bias[0]), _pad_lanes(ssd_a_log[0]),
        jnp.repeat(ssd_d[0], SSD_HEAD_DIM).reshape(1, -1), ssd_norm_g[0].reshape(1, -1),
        conf_dw_w[0], conf_dw_b[0].reshape(1, -1), conf_ln_g[0].reshape(1, -1), conf_ln_b[0].reshape(1, -1),
        w_out[0].astype(BF16), norm2_g[0].reshape(1, -1), bsz, seq, ts)

    keys = peer_sub_keys[0].reshape(PEER_HEADS * 2, PEER_N_KEYS, PEER_D_HALF).astype(BF16)
    idx_t, gate_t = _route_call(hn2, peer_w_query[0].astype(BF16), keys, tb)

    w_t = _peer_u_call(idx_t, hn2.reshape(t, SUBLANES, LANES), gate_t, _pack_table(peer_u[0]), tb)
    peer = _peer_v_call(idx_t, w_t, _pack_table(peer_v[0]), tb)

    out = _final_call(h1, peer.reshape(t, d), mod3, final_norm_g, seq, tm)
    return out.reshape(bsz, seq, d)
```

```python
import functools
import math

import jax
import jax.numpy as jnp
from jax import lax
from jax.experimental import pallas as pl
from jax.experimental.pallas import tpu as pltpu

F32 = jnp.float32
BF16 = jnp.bfloat16
HIGHEST = lax.Precision.HIGHEST

D_MODEL = 1024
CHUNK = 64
SSD_WIDTH = 512
SSD_HEADS = 8
SSD_HEAD_DIM = 64
SSD_GROUPS = 2
SSD_STATE = 128
SSD_CONV = 4
SSD_XBC = 1024
CONF_WIDTH = 512
CONF_CONV = 31
PEER_HEADS = 8
PEER_N_KEYS = 128
PEER_D_HALF = 128
PEER_TOPK = 16
PEER_SLOTS = PEER_HEADS * PEER_TOPK
NORM_EPS = 1e-6

LANES = 128
SUBLANES = 8
ROW_SUBLANES = D_MODEL // 2 // LANES
VMEM_LIMIT = 56 * 1024 * 1024

XBC_TAIL = 8
GLU_TAIL = 32


def _silu(v):
    return v * jax.nn.sigmoid(v)


def _softplus(v):
    return jnp.maximum(v, 0.0) + jnp.log(1.0 + jnp.exp(-jnp.abs(v)))


def _bdot(a, b):
    return jnp.dot(a.astype(BF16), b.astype(BF16), preferred_element_type=F32)


def _mod_kernel(c_ref, w_ref, b_ref, o_ref):
    cond = _silu(c_ref[...])
    o_ref[...] = jnp.dot(cond, w_ref[...], precision=HIGHEST, preferred_element_type=F32) + b_ref[...]


def _mod_call(c, ada_w, ada_b):
    bsz, d = c.shape
    n = ada_w.shape[1]
    return pl.pallas_call(
        _mod_kernel,
        out_shape=jax.ShapeDtypeStruct((bsz, n), F32),
        grid=(n // d,),
        in_specs=[pl.BlockSpec((bsz, d), lambda i: (0, 0)),
                  pl.BlockSpec((d, d), lambda i: (0, i)),
                  pl.BlockSpec((1, d), lambda i: (0, i))],
        out_specs=pl.BlockSpec((bsz, d), lambda i: (0, i)),
        name="mod",
    )(c, ada_w, ada_b.reshape(1, n))


def _inproj_kernel(x_ref, mod_ref, g_ref, wz_ref, wx_ref, wg_ref, wd_ref,
                   z_ref, xbc_ref, glu_ref, dt_ref):
    x = x_ref[...]
    ms = jnp.mean(x * x, axis=-1, keepdims=True)
    y = x * lax.rsqrt(ms + NORM_EPS) * g_ref[...]
    sh = mod_ref[0, 0:1, :]
    sc = mod_ref[0, 1:2, :]
    hn = (y * (1.0 + sc) + sh).astype(BF16)
    z_ref[...] = jnp.dot(hn, wz_ref[...], preferred_element_type=F32)
    xbc_ref[...] = jnp.dot(hn, wx_ref[...], preferred_element_type=F32)
    glu_ref[...] = jnp.dot(hn, wg_ref[...], preferred_element_type=F32)
    dt_ref[...] = jnp.dot(hn, wd_ref[...], preferred_element_type=F32)


def _inproj_call(x2, mod3, norm1_g, wz, wx, wg, wd, seq, tm):
    t, d = x2.shape
    per_b = seq // tm
    const = lambda i: (0, 0)
    row = lambda i: (i, 0)
    return pl.pallas_call(
        _inproj_kernel,
        out_shape=(jax.ShapeDtypeStruct((t, SSD_WIDTH), F32),
                   jax.ShapeDtypeStruct((t, SSD_XBC), F32),
                   jax.ShapeDtypeStruct((t, 2 * CONF_WIDTH), F32),
                   jax.ShapeDtypeStruct((t, LANES), F32)),
        grid=(t // tm,),
        in_specs=[pl.BlockSpec((tm, d), row),
                  pl.BlockSpec((1, 6, d), lambda i: (i // per_b, 0, 0)),
                  pl.BlockSpec((1, d), const),
                  pl.BlockSpec(wz.shape, const),
                  pl.BlockSpec(wx.shape, const),
                  pl.BlockSpec(wg.shape, const),
                  pl.BlockSpec(wd.shape, const)],
        out_specs=(pl.BlockSpec((tm, SSD_WIDTH), row),
                   pl.BlockSpec((tm, SSD_XBC), row),
                   pl.BlockSpec((tm, 2 * CONF_WIDTH), row),
                   pl.BlockSpec((tm, LANES), row)),
        compiler_params=pltpu.CompilerParams(vmem_limit_bytes=VMEM_LIMIT),
        name="inproj",
    )(x2, mod3, norm1_g.reshape(1, d), wz, wx, wg, wd)


def _mixer_kernel(x_ref, z_ref, xbc_ref, glu_ref, dt_ref, mod_ref,
                  cw_ref, cb_ref, dtb_ref, alog_ref, dexp_ref, sng_ref,
                  dww_ref, dwb_ref, lng_ref, lnb_ref, wout_ref, n2g_ref,
                  h1_ref, hn2_ref,
                  xext_ref, gext_ref, hst_ref, xc_ref, xdt_ref, acs_ref, eacs_ref,
                  acst_ref, bmt_ref, y_ref, *, ts):
    nc = ts // CHUNK
    hw = SSD_WIDTH // SSD_GROUPS

    @pl.when(pl.program_id(1) == 0)
    def _():
        xext_ref[0:XBC_TAIL, :] = jnp.zeros((XBC_TAIL, SSD_XBC), F32)
        gext_ref[0:GLU_TAIL, :] = jnp.zeros((GLU_TAIL, CONF_WIDTH), F32)
        hst_ref[...] = jnp.zeros(hst_ref.shape, F32)

    xext_ref[XBC_TAIL:XBC_TAIL + ts, :] = xbc_ref[...]
    acc = cb_ref[...] + cw_ref[0:1, :] * xext_ref[pl.ds(XBC_TAIL - SSD_CONV + 1, ts), :]
    for k in range(1, SSD_CONV):
        acc = acc + cw_ref[k:k + 1, :] * xext_ref[pl.ds(XBC_TAIL - SSD_CONV + 1 + k, ts), :]
    xext_ref[0:XBC_TAIL, :] = xext_ref[ts:ts + XBC_TAIL, :]
    xc_ref[...] = _silu(acc)

    dt = _softplus(dt_ref[...] + dtb_ref[...])
    dta = dt * (-jnp.exp(alog_ref[...]))
    ri = lax.broadcasted_iota(jnp.int32, (ts, ts), 0)
    ci = lax.broadcasted_iota(jnp.int32, (ts, ts), 1)
    ltri = jnp.where(((ri // CHUNK) == (ci // CHUNK)) & (ci <= ri), 1.0, 0.0).astype(F32)
    acs = jnp.dot(ltri, dta, precision=HIGHEST, preferred_element_type=F32)
    acst_ref[...] = acs.T
    er = lax.broadcasted_iota(jnp.int32, (LANES, SSD_WIDTH), 0)
    ec = lax.broadcasted_iota(jnp.int32, (LANES, SSD_WIDTH), 1)
    expand = jnp.where((ec // SSD_HEAD_DIM) == er, 1.0, 0.0).astype(F32)
    dt_exp = jnp.dot(dt, expand, precision=HIGHEST, preferred_element_type=F32)
    acs_exp = jnp.dot(acs, expand, precision=HIGHEST, preferred_element_type=F32)
    acs_ref[...] = acs_exp
    eacs_ref[...] = jnp.exp(acs_exp)
    xdt_ref[...] = xc_ref[:, 0:SSD_WIDTH] * dt_exp
    bmt_ref[...] = xc_ref[:, SSD_WIDTH:SSD_WIDTH + SSD_GROUPS * SSD_STATE].T

    tr = lax.broadcasted_iota(jnp.int32, (CHUNK, CHUNK), 0)
    tc = lax.broadcasted_iota(jnp.int32, (CHUNK, CHUNK), 1)
    tril = tc <= tr
    c_off = SSD_WIDTH + SSD_GROUPS * SSD_STATE

    for c in range(nc):
        r0 = c * CHUNK
        rows = slice(r0, r0 + CHUNK)
        a_last = acs_ref[r0 + CHUNK - 1:r0 + CHUNK, :]
        xw = xdt_ref[rows, :] * jnp.exp(a_last - acs_ref[rows, :])
        cdec = jnp.exp(a_last)
        y_parts = []
        for g in range(SSD_GROUPS):
            cg = xc_ref[rows, c_off + g * SSD_STATE:c_off + (g + 1) * SSD_STATE].astype(BF16)
            bg = xc_ref[rows, SSD_WIDTH + g * SSD_STATE:SSD_WIDTH + (g + 1) * SSD_STATE].astype(BF16)
            cb = lax.dot_general(cg, bg, (((1,), (1,)), ((), ())), preferred_element_type=F32)
            hg = hst_ref[g]
            yoff = jnp.dot(cg, hg.astype(BF16), preferred_element_type=F32)
            st = _bdot(bmt_ref[g * SSD_STATE:(g + 1) * SSD_STATE, rows], xw[:, g * hw:(g + 1) * hw])
            hst_ref[g] = hg * cdec[:, g * hw:(g + 1) * hw] + st
            yds = []
            for hh in range(SSD_HEADS // SSD_GROUPS):
                h = g * (SSD_HEADS // SSD_GROUPS) + hh
                cols = slice(h * SSD_HEAD_DIM, (h + 1) * SSD_HEAD_DIM)
                seg = acs_ref[rows, cols] - acst_ref[h:h + 1, rows]
                dec = jnp.exp(jnp.where(tril, seg, -jnp.inf))
                yds.append(_bdot(cb * dec, xdt_ref[rows, cols]))
            y_parts.append(jnp.concatenate(yds, axis=1) + yoff * eacs_ref[rows, g * hw:(g + 1) * hw])
        y_ref[rows, :] = jnp.concatenate(y_parts, axis=1) + dexp_ref[...] * xc_ref[rows, 0:SSD_WIDTH]

    y = y_ref[...] * _silu(z_ref[...])
    y_ssd = y * lax.rsqrt(jnp.mean(y * y, axis=-1, keepdims=True) + NORM_EPS) * sng_ref[...]

    gext_ref[GLU_TAIL:GLU_TAIL + ts, :] = glu_ref[:, 0:CONF_WIDTH] * jax.nn.sigmoid(glu_ref[:, CONF_WIDTH:])
    base = GLU_TAIL - CONF_CONV + 1
    u = dwb_ref[...] + dww_ref[0:1, :] * gext_ref[pl.ds(base, ts), :]
    for k in range(1, CONF_CONV):
        u = u + dww_ref[k:k + 1, :] * gext_ref[pl.ds(base + k, ts), :]
    gext_ref[0:GLU_TAIL, :] = gext_ref[ts:ts + GLU_TAIL, :]
    mu = jnp.mean(u, axis=-1, keepdims=True)
    uc = u - mu
    var = jnp.mean(uc * uc, axis=-1, keepdims=True)
    y_conf = _silu(uc * lax.rsqrt(var + NORM_EPS) * lng_ref[...] + lnb_ref[...])

    mix = (jnp.dot(y_ssd.astype(BF16), wout_ref[0:SSD_WIDTH, :], preferred_element_type=F32)
           + jnp.dot(y_conf.astype(BF16), wout_ref[SSD_WIDTH:, :], preferred_element_type=F32))
    h1 = x_ref[...] + mod_ref[0, 2:3, :] * mix
    h1_ref[...] = h1
    hn = h1 * lax.rsqrt(jnp.mean(h1 * h1, axis=-1, keepdims=True) + NORM_EPS) * n2g_ref[...]
    hn2_ref[...] = hn * (1.0 + mod_ref[0, 4:5, :]) + mod_ref[0, 3:4, :]


def _mixer_call(x2, z, xbc, glu, dt, mod3, cw, cb, dtb, alog, dexp, sng, dww, dwb, lng, lnb,
                wout, n2g, bsz, seq, ts):
    t, d = x2.shape
    per_b = seq // ts
    row = lambda b, j: (b * per_b + j, 0)
    const = lambda b, j: (0, 0)

    def full(a):
        return pl.BlockSpec(a.shape, const)

    return pl.pallas_call(
        functools.partial(_mixer_kernel, ts=ts),
        out_shape=(jax.ShapeDtypeStruct((t, d), F32), jax.ShapeDtypeStruct((t, d), F32)),
        grid=(bsz, per_b),
        in_specs=[pl.BlockSpec((ts, d), row),
                  pl.BlockSpec((ts, SSD_WIDTH), row),
                  pl.BlockSpec((ts, SSD_XBC), row),
                  pl.BlockSpec((ts, 2 * CONF_WIDTH), row),
                  pl.BlockSpec((ts, LANES), row),
                  pl.BlockSpec((1, 6, d), lambda b, j: (b, 0, 0)),
                  full(cw), full(cb), full(dtb), full(alog), full(dexp), full(sng),
                  full(dww), full(dwb), full(lng), full(lnb), full(wout), full(n2g)],
        out_specs=(pl.BlockSpec((ts, d), row), pl.BlockSpec((ts, d), row)),
        scratch_shapes=[pltpu.VMEM((ts + XBC_TAIL, SSD_XBC), F32),
                        pltpu.VMEM((ts + GLU_TAIL, CONF_WIDTH), F32),
                        pltpu.VMEM((SSD_GROUPS, SSD_STATE, SSD_WIDTH // SSD_GROUPS), F32),
                        pltpu.VMEM((ts, SSD_XBC), F32),
                        pltpu.VMEM((ts, SSD_WIDTH), F32),
                        pltpu.VMEM((ts, SSD_WIDTH), F32),
                        pltpu.VMEM((ts, SSD_WIDTH), F32),
                        pltpu.VMEM((LANES, ts), F32),
                        pltpu.VMEM((SSD_GROUPS * SSD_STATE, ts), F32),
                        pltpu.VMEM((ts, SSD_WIDTH), F32)],
        compiler_params=pltpu.CompilerParams(
            dimension_semantics=("arbitrary", "arbitrary"), vmem_limit_bytes=VMEM_LIMIT),
        name="mixer",
    )(x2, z, xbc, glu, dt, mod3, cw, cb, dtb, alog, dexp, sng, dww, dwb, lng, lnb, wout, n2g)


_PAIR_COUNTS = tuple(PEER_TOPK // (a + 1) for a in range(PEER_TOPK))


def _topk_rows(s, k):
    n = s.shape[0]
    iota = lax.broadcasted_iota(jnp.int32, s.shape, 0).astype(F32)
    vals, rows = [], []
    for _ in range(k):
        m = jnp.max(s, axis=0, keepdims=True)
        am = jnp.min(jnp.where(s == m, iota, float(n)), axis=0, keepdims=True)
        vals.append(m)
        rows.append(am)
        s = jnp.where(iota == am, -jnp.inf, s)
    return vals, rows


def _route_kernel(hn_ref, wq_ref, keys_ref, idx_ref, gate_ref):
    q = jnp.dot(hn_ref[...].astype(BF16), wq_ref[...], preferred_element_type=F32)
    tm = q.shape[0]
    nt = (((1,), (1,)), ((), ()))
    n_cand = sum(_PAIR_COUNTS)
    n_pad = -n_cand % SUBLANES
    rows_out = []
    for h in range(PEER_HEADS):
        sv, si = [], []
        for i in range(2):
            col = (h * 2 + i) * PEER_D_HALF
            qh = q[:, col:col + PEER_D_HALF].astype(BF16)
            sc = lax.dot_general(keys_ref[h * 2 + i], qh, nt, preferred_element_type=F32)
            v, r = _topk_rows(sc, PEER_TOPK)
            sv.append(v)
            si.append(r)
        sv2 = jnp.concatenate(sv[1], axis=0)
        si2 = jnp.concatenate(si[1], axis=0)
        cand = jnp.concatenate([sv[0][a] + sv2[0:nb] for a, nb in enumerate(_PAIR_COUNTS)]
                               + [jnp.full((n_pad, tm), -jnp.inf, F32)], axis=0)
        cidx = jnp.concatenate([si[0][a] * float(PEER_N_KEYS) + si2[0:nb] for a, nb in enumerate(_PAIR_COUNTS)]
                               + [jnp.zeros((n_pad, tm), F32)], axis=0)
        iota = lax.broadcasted_iota(jnp.int32, cand.shape, 0).astype(F32)
        best, eidx = [], []
        for _ in range(PEER_TOPK):
            m = jnp.max(cand, axis=0, keepdims=True)
            p = jnp.min(jnp.where(cand == m, iota, float(cand.shape[0])), axis=0, keepdims=True)
            hit = iota == p
            best.append(m)
            eidx.append(jnp.max(jnp.where(hit, cidx, -1.0), axis=0, keepdims=True))
            cand = jnp.where(hit, -jnp.inf, cand)
        best = jnp.concatenate(best, axis=0)
        e = jnp.exp(best - best[0:1, :])
        gate_ref[h * PEER_TOPK:(h + 1) * PEER_TOPK, :] = e / jnp.sum(e, axis=0, keepdims=True)
        rows_out.append(jnp.concatenate(eidx, axis=0))
    expert = jnp.concatenate(rows_out, axis=0)
    idx_ref[...] = (expert.T * float(ROW_SUBLANES)).astype(jnp.int32)


def _route_call(hn2, wq, keys, tm):
    t, d = hn2.shape
    return pl.pallas_call(
        _route_kernel,
        out_shape=(jax.ShapeDtypeStruct((t, PEER_SLOTS), jnp.int32),
                   jax.ShapeDtypeStruct((PEER_SLOTS, t), F32)),
        grid=(t // tm,),
        in_specs=[pl.BlockSpec((tm, d), lambda i: (i, 0)),
                  pl.BlockSpec(wq.shape, lambda i: (0, 0)),
                  pl.BlockSpec(keys.shape, lambda i: (0, 0, 0))],
        out_specs=(pl.BlockSpec((tm, PEER_SLOTS), lambda i: (i, 0)),
                   pl.BlockSpec((PEER_SLOTS, tm), lambda i: (0, i))),
        compiler_params=pltpu.CompilerParams(vmem_limit_bytes=VMEM_LIMIT),
        name="route",
    )(hn2, wq, keys)


def _pack_kernel(t_ref, o_ref):
    half = t_ref.shape[1] // 2
    lo = pltpu.bitcast(t_ref[:, :half].astype(BF16).astype(F32), jnp.uint32)
    hi = pltpu.bitcast(t_ref[:, half:].astype(BF16).astype(F32), jnp.uint32)
    word = hi | (lo >> 16)
    rows = t_ref.shape[0]
    for s in range(ROW_SUBLANES):
        o_ref[pl.ds(s, rows, stride=ROW_SUBLANES), :] = word[:, s * LANES:(s + 1) * LANES]


def _pack_table(tbl):
    n, d = tbl.shape
    rows = 512
    return pl.pallas_call(
        _pack_kernel,
        out_shape=jax.ShapeDtypeStruct((n * ROW_SUBLANES, LANES), jnp.uint32),
        grid=(n // rows,),
        in_specs=[pl.BlockSpec((rows, d), lambda i: (i, 0))],
        out_specs=pl.BlockSpec((rows * ROW_SUBLANES, LANES), lambda i: (i, 0)),
        name="pack",
    )(tbl)


def _unpack_lo(w):
    return pltpu.bitcast(w << 16, F32)


def _unpack_hi(w):
    return pltpu.bitcast(w & jnp.uint32(0xFFFF0000), F32)


def _gather_rows(idx_ref, tbl_ref, slot_ref, t):
    for j in range(PEER_SLOTS):
        start = pl.multiple_of(idx_ref[t, j], ROW_SUBLANES)
        slot_ref[j * ROW_SUBLANES:(j + 1) * ROW_SUBLANES, :] = tbl_ref[pl.ds(start, ROW_SUBLANES), :]


def _peer_u_kernel(idx_ref, x_ref, gate_ref, tbl_ref, w_ref, slot_a, slot_b, prod_a, prod_b, *, tb):
    per_vreg = SUBLANES // ROW_SUBLANES
    lane = lax.broadcasted_iota(jnp.int32, (PEER_SLOTS, LANES), 1)

    def products(slot_ref, prod_ref, t):
        xt = x_ref[t]
        xlo = jnp.concatenate([xt[0:ROW_SUBLANES]] * per_vreg, axis=0)
        xhi = jnp.concatenate([xt[ROW_SUBLANES:]] * per_vreg, axis=0)
        words = slot_ref[...].reshape(PEER_SLOTS // per_vreg, SUBLANES, LANES)
        prod = _unpack_lo(words) * xlo[None] + _unpack_hi(words) * xhi[None]
        prod_ref[...] = prod.reshape(PEER_SLOTS * ROW_SUBLANES, LANES)

    def reduce_into(prod_ref, tl, acc):
        part = prod_ref[pl.ds(0, PEER_SLOTS, stride=ROW_SUBLANES), :]
        for r in range(1, ROW_SUBLANES):
            part = part + prod_ref[pl.ds(r, PEER_SLOTS, stride=ROW_SUBLANES), :]
        col = jnp.sum(part, axis=-1, keepdims=True)
        return jnp.where(lane == tl, col, acc)

    prod_b[...] = jnp.zeros(prod_b.shape, F32)
    for blk in range(tb // LANES):
        base = blk * LANES
        _gather_rows(idx_ref, tbl_ref, slot_a, base)

        def body(i, acc):
            t0 = base + 2 * i
            acc = reduce_into(prod_b, 2 * i - 1, acc)
            products(slot_a, prod_a, t0)
            _gather_rows(idx_ref, tbl_ref, slot_b, t0 + 1)
            acc = reduce_into(prod_a, 2 * i, acc)
            products(slot_b, prod_b, t0 + 1)
            _gather_rows(idx_ref, tbl_ref, slot_a, jnp.minimum(t0 + 2, tb - 1))
            return acc

        act = lax.fori_loop(0, LANES // 2, body, jnp.zeros((PEER_SLOTS, LANES), F32))
        act = reduce_into(prod_b, LANES - 1, act)
        gelu = 0.5 * act * (1.0 + lax.erf(act * (1.0 / math.sqrt(2.0))))
        w_ref[blk * LANES:(blk + 1) * LANES, :] = (gate_ref[:, blk * LANES:(blk + 1) * LANES] * gelu).T


def _peer_v_kernel(idx_ref, w_ref, tbl_ref, o_ref, slot_a, slot_b, wrep_hi, wrep_lo, *, tb):
    cols = 2 * ROW_SUBLANES * PEER_SLOTS
    w = w_ref[...]
    hi = w.astype(BF16)
    lo = (w - hi.astype(F32)).astype(BF16)
    jr = lax.broadcasted_iota(jnp.int32, (PEER_SLOTS, cols), 0)
    jc = lax.broadcasted_iota(jnp.int32, (PEER_SLOTS, cols), 1)
    expand = jnp.where(jc // (2 * ROW_SUBLANES) == jr, 1.0, 0.0).astype(BF16)
    wrep_hi[...] = jnp.dot(hi, expand, preferred_element_type=F32)
    wrep_lo[...] = jnp.dot(lo, expand, preferred_element_type=F32)
    rr = lax.broadcasted_iota(jnp.int32, (SUBLANES, cols), 0)
    rc = lax.broadcasted_iota(jnp.int32, (SUBLANES, cols), 1)
    mask = (rc % (2 * ROW_SUBLANES)) == 2 * (rr % ROW_SUBLANES) + rr // ROW_SUBLANES

    def lhs_rows(t):
        return [jnp.where(mask, jnp.broadcast_to(ref[pl.ds(t, 1), :], (SUBLANES, cols)), 0.0)
                for ref in (wrep_hi, wrep_lo)]

    def gather_pair(slot_ref, t):
        for k in range(2):
            for j in range(PEER_SLOTS):
                start = pl.multiple_of(idx_ref[t + k, j], ROW_SUBLANES)
                slot_ref[j * ROW_SUBLANES:(j + 1) * ROW_SUBLANES, k * LANES:(k + 1) * LANES] = (
                    tbl_ref[pl.ds(start, ROW_SUBLANES), :])

    def combine_pair(slot_ref, t):
        lhs = jnp.concatenate(lhs_rows(t) + lhs_rows(t + 1), axis=0).astype(BF16)
        res = jnp.dot(lhs, pltpu.bitcast(slot_ref[...], BF16), preferred_element_type=F32)
        o_ref[t] = res[0:SUBLANES, 0:LANES] + res[SUBLANES:2 * SUBLANES, 0:LANES]
        o_ref[t + 1] = res[2 * SUBLANES:3 * SUBLANES, LANES:] + res[3 * SUBLANES:, LANES:]

    pairs = slot_a.shape[0]
    half = 2 * pairs
    for q in range(pairs):
        gather_pair(slot_a.at[q], 2 * q)

    def body(i, carry):
        t0 = 2 * half * i
        for q in range(pairs):
            combine_pair(slot_a.at[q], t0 + 2 * q)
        for q in range(pairs):
            gather_pair(slot_b.at[q], t0 + half + 2 * q)
        for q in range(pairs):
            combine_pair(slot_b.at[q], t0 + half + 2 * q)
        for q in range(pairs):
            gather_pair(slot_a.at[q], jnp.minimum(t0 + 2 * half + 2 * q, tb - 2))
        return carry

    lax.fori_loop(0, tb // (2 * half), body, 0)


def _table_spec(tbl):
    return pl.BlockSpec(tbl.shape, lambda i: (0, 0), pipeline_mode=pl.Buffered(1))


def _peer_u_call(idx_t, x3, gate_t, tbl, tb):
    t = x3.shape[0]
    return pl.pallas_call(
        functools.partial(_peer_u_kernel, tb=tb),
        out_shape=jax.ShapeDtypeStruct((t, PEER_SLOTS), F32),
        grid=(t // tb,),
        in_specs=[pl.BlockSpec((tb, PEER_SLOTS), lambda i: (i, 0), memory_space=pltpu.SMEM),
                  pl.BlockSpec((tb, SUBLANES, LANES), lambda i: (i, 0, 0)),
                  pl.BlockSpec((PEER_SLOTS, tb), lambda i: (0, i)),
                  _table_spec(tbl)],
        out_specs=pl.BlockSpec((tb, PEER_SLOTS), lambda i: (i, 0)),
        scratch_shapes=[pltpu.VMEM((PEER_SLOTS * ROW_SUBLANES, LANES), jnp.uint32),
                        pltpu.VMEM((PEER_SLOTS * ROW_SUBLANES, LANES), jnp.uint32),
                        pltpu.VMEM((PEER_SLOTS * ROW_SUBLANES, LANES), F32),
                        pltpu.VMEM((PEER_SLOTS * ROW_SUBLANES, LANES), F32)],
        compiler_params=pltpu.CompilerParams(vmem_limit_bytes=VMEM_LIMIT),
        name="peer_u",
    )(idx_t, x3, gate_t, tbl)


def _peer_v_call(idx_t, w_t, tbl, tb):
    t = idx_t.shape[0]
    return pl.pallas_call(
        functools.partial(_peer_v_kernel, tb=tb),
        out_shape=jax.ShapeDtypeStruct((t, SUBLANES, LANES), F32),
        grid=(t // tb,),
        in_specs=[pl.BlockSpec((tb, PEER_SLOTS), lambda i: (i, 0), memory_space=pltpu.SMEM),
                  pl.BlockSpec((tb, PEER_SLOTS), lambda i: (i, 0)),
                  _table_spec(tbl)],
        out_specs=pl.BlockSpec((tb, SUBLANES, LANES), lambda i: (i, 0, 0)),
        scratch_shapes=[pltpu.VMEM((2, PEER_SLOTS * ROW_SUBLANES, 2 * LANES), jnp.uint32),
                        pltpu.VMEM((2, PEER_SLOTS * ROW_SUBLANES, 2 * LANES), jnp.uint32),
                        pltpu.VMEM((tb, 2 * ROW_SUBLANES * PEER_SLOTS), F32),
                        pltpu.VMEM((tb, 2 * ROW_SUBLANES * PEER_SLOTS), F32)],
        compiler_params=pltpu.CompilerParams(vmem_limit_bytes=VMEM_LIMIT),
        name="peer_v",
    )(idx_t, w_t, tbl)


def _final_kernel(h1_ref, p_ref, mod_ref, g_ref, o_ref):
    h = h1_ref[...] + mod_ref[0, 5:6, :] * p_ref[...]
    o_ref[...] = h * lax.rsqrt(jnp.mean(h * h, axis=-1, keepdims=True) + NORM_EPS) * g_ref[...]


def _final_call(h1, peer, mod3, g, seq, tm):
    t, d = h1.shape
    per_b = seq // tm
    row = lambda i: (i, 0)
    return pl.pallas_call(
        _final_kernel,
        out_shape=jax.ShapeDtypeStruct((t, d), F32),
        grid=(t // tm,),
        in_specs=[pl.BlockSpec((tm, d), row), pl.BlockSpec((tm, d), row),
                  pl.BlockSpec((1, 6, d), lambda i: (i // per_b, 0, 0)),
                  pl.BlockSpec((1, d), lambda i: (0, 0))],
        out_specs=pl.BlockSpec((tm, d), row),
        name="final",
    )(h1, peer, mod3, g.reshape(1, d))


def _pad_lanes(v):
    return jnp.pad(v.reshape(1, -1), ((0, 0), (0, LANES - v.shape[-1])))


def kernel(x, c, ada_w, ada_b, norm1_g, w_in, ssd_conv_w, ssd_conv_b, ssd_dt_bias, ssd_a_log, ssd_d, ssd_norm_g, conf_dw_w, conf_dw_b, conf_ln_g, conf_ln_b, w_out, norm2_g, peer_w_query, peer_sub_keys, peer_u, peer_v, final_norm_g):
    bsz, seq, d = x.shape
    assert d == D_MODEL and ada_w.shape[0] == 1
    t = bsz * seq
    tm = min(512, seq)
    ts = min(256, seq)
    tb = min(256, seq)
    x2 = x.reshape(t, d)

    mod3 = _mod_call(c, ada_w[0], ada_b[0]).reshape(bsz, 6, d)

    wi = w_in[0]
    o1 = SSD_WIDTH
    o2 = o1 + SSD_XBC
    o3 = o2 + SSD_HEADS
    wz = wi[:, :o1].astype(BF16)
    wx = wi[:, o1:o2].astype(BF16)
    wd = jnp.pad(wi[:, o2:o3], ((0, 0), (0, LANES - SSD_HEADS))).astype(BF16)
    wg = wi[:, o3:].astype(BF16)
    z, xbc, glu, dt = _inproj_call(x2, mod3, norm1_g[0], wz, wx, wg, wd, seq, tm)

    h1, hn2 = _mixer_call(
        x2, z, xbc, glu, dt, mod3,
        ssd_conv_w[0], ssd_conv_b[0].reshape(1, -1), _pad_lanes(ssd_dt_bias[0]), _pad_lanes(ssd_a_log[0]),
        jnp.repeat(ssd_d[0], SSD_HEAD_DIM).reshape(1, -1), ssd_norm_g[0].reshape(1, -1),
        conf_dw_w[0], conf_dw_b[0].reshape(1, -1), conf_ln_g[0].reshape(1, -1), conf_ln_b[0].reshape(1, -1),
        w_out[0].astype(BF16), norm2_g[0].reshape(1, -1), bsz, seq, ts)

    keys = peer_sub_keys[0].reshape(PEER_HEADS * 2, PEER_N_KEYS, PEER_D_HALF).astype(BF16)
    idx_t, gate_t = _route_call(hn2, peer_w_query[0].astype(BF16), keys, tb)

    w_t = _peer_u_call(idx_t, hn2.reshape(t, SUBLANES, LANES), gate_t, _pack_table(peer_u[0]), tb)
    peer = _peer_v_call(idx_t, w_t, _pack_table(peer_v[0]), tb)

    out = _final_call(h1, peer.reshape(t, d), mod3, final_norm_g, seq, tm)
    return out.reshape(bsz, seq, d)
```

```python
import functools
import math

import jax
import jax.numpy as jnp
from jax import lax
from jax.experimental import pallas as pl
from jax.experimental.pallas import tpu as pltpu

F32 = jnp.float32
BF16 = jnp.bfloat16
HIGHEST = lax.Precision.HIGHEST

D_MODEL = 1024
CHUNK = 64
SSD_WIDTH = 512
SSD_HEADS = 8
SSD_HEAD_DIM = 64
SSD_GROUPS = 2
SSD_STATE = 128
SSD_CONV = 4
SSD_XBC = 1024
CONF_WIDTH = 512
CONF_CONV = 31
PEER_HEADS = 8
PEER_N_KEYS = 128
PEER_D_HALF = 128
PEER_TOPK = 16
PEER_SLOTS = PEER_HEADS * PEER_TOPK
NORM_EPS = 1e-6

LANES = 128
SUBLANES = 8
ROW_SUBLANES = D_MODEL // 2 // LANES
VMEM_LIMIT = 56 * 1024 * 1024

XBC_TAIL = 8
GLU_TAIL = 32


def _silu(v):
    return v * jax.nn.sigmoid(v)


def _softplus(v):
    return jnp.maximum(v, 0.0) + jnp.log(1.0 + jnp.exp(-jnp.abs(v)))


def _bdot(a, b):
    return jnp.dot(a.astype(BF16), b.astype(BF16), preferred_element_type=F32)


def _mod_kernel(c_ref, w_ref, b_ref, o_ref):
    cond = _silu(c_ref[...])
    o_ref[...] = jnp.dot(cond, w_ref[...], precision=HIGHEST, preferred_element_type=F32) + b_ref[...]


def _mod_call(c, ada_w, ada_b):
    bsz, d = c.shape
    n = ada_w.shape[1]
    return pl.pallas_call(
        _mod_kernel,
        out_shape=jax.ShapeDtypeStruct((bsz, n), F32),
        grid=(n // d,),
        in_specs=[pl.BlockSpec((bsz, d), lambda i: (0, 0)),
                  pl.BlockSpec((d, d), lambda i: (0, i)),
                  pl.BlockSpec((1, d), lambda i: (0, i))],
        out_specs=pl.BlockSpec((bsz, d), lambda i: (0, i)),
        name="mod",
    )(c, ada_w, ada_b.reshape(1, n))


def _inproj_kernel(x_ref, mod_ref, g_ref, wz_ref, wx_ref, wg_ref, wd_ref,
                   z_ref, xbc_ref, glu_ref, dt_ref):
    x = x_ref[...]
    ms = jnp.mean(x * x, axis=-1, keepdims=True)
    y = x * lax.rsqrt(ms + NORM_EPS) * g_ref[...]
    sh = mod_ref[0, 0:1, :]
    sc = mod_ref[0, 1:2, :]
    hn = (y * (1.0 + sc) + sh).astype(BF16)
    z_ref[...] = jnp.dot(hn, wz_ref[...], preferred_element_type=F32)
    xbc_ref[...] = jnp.dot(hn, wx_ref[...], preferred_element_type=F32)
    glu_ref[...] = jnp.dot(hn, wg_ref[...], preferred_element_type=F32)
    dt_ref[...] = jnp.dot(hn, wd_ref[...], preferred_element_type=F32)


def _inproj_call(x2, mod3, norm1_g, wz, wx, wg, wd, seq, tm):
    t, d = x2.shape
    per_b = seq // tm
    const = lambda i: (0, 0)
    row = lambda i: (i, 0)
    return pl.pallas_call(
        _inproj_kernel,
        out_shape=(jax.ShapeDtypeStruct((t, SSD_WIDTH), F32),
                   jax.ShapeDtypeStruct((t, SSD_XBC), F32),
                   jax.ShapeDtypeStruct((t, 2 * CONF_WIDTH), F32),
                   jax.ShapeDtypeStruct((t, LANES), F32)),
        grid=(t // tm,),
        in_specs=[pl.BlockSpec((tm, d), row),
                  pl.BlockSpec((1, 6, d), lambda i: (i // per_b, 0, 0)),
                  pl.BlockSpec((1, d), const),
                  pl.BlockSpec(wz.shape, const),
                  pl.BlockSpec(wx.shape, const),
                  pl.BlockSpec(wg.shape, const),
                  pl.BlockSpec(wd.shape, const)],
        out_specs=(pl.BlockSpec((tm, SSD_WIDTH), row),
                   pl.BlockSpec((tm, SSD_XBC), row),
                   pl.BlockSpec((tm, 2 * CONF_WIDTH), row),
                   pl.BlockSpec((tm, LANES), row)),
        compiler_params=pltpu.CompilerParams(vmem_limit_bytes=VMEM_LIMIT),
        name="inproj",
    )(x2, mod3, norm1_g.reshape(1, d), wz, wx, wg, wd)


def _mixer_kernel(x_ref, z_ref, xbc_ref, glu_ref, dt_ref, mod_ref,
                  cw_ref, cb_ref, dtb_ref, alog_ref, dexp_ref, sng_ref,
                  dww_ref, dwb_ref, lng_ref, lnb_ref, wout_ref, n2g_ref,
                  h1_ref, hn2_ref,
                  xext_ref, gext_ref, hst_ref, xc_ref, xdt_ref, acs_ref, eacs_ref,
                  acst_ref, bmt_ref, y_ref, gsh_ref, *, ts):
    nc = ts // CHUNK
    hw = SSD_WIDTH // SSD_GROUPS

    @pl.when(pl.program_id(1) == 0)
    def _():
        xext_ref[0:XBC_TAIL, :] = jnp.zeros((XBC_TAIL, SSD_XBC), F32)
        gext_ref[0:GLU_TAIL, :] = jnp.zeros((GLU_TAIL, CONF_WIDTH), F32)
        hst_ref[...] = jnp.zeros(hst_ref.shape, F32)

    xext_ref[XBC_TAIL:XBC_TAIL + ts, :] = xbc_ref[...]
    acc = cb_ref[...] + cw_ref[0:1, :] * xext_ref[pl.ds(XBC_TAIL - SSD_CONV + 1, ts), :]
    for k in range(1, SSD_CONV):
        acc = acc + cw_ref[k:k + 1, :] * xext_ref[pl.ds(XBC_TAIL - SSD_CONV + 1 + k, ts), :]
    xext_ref[0:XBC_TAIL, :] = xext_ref[ts:ts + XBC_TAIL, :]
    xc_ref[...] = _silu(acc)

    dt = _softplus(dt_ref[...] + dtb_ref[...])
    dta = dt * (-jnp.exp(alog_ref[...]))
    ri = lax.broadcasted_iota(jnp.int32, (ts, ts), 0)
    ci = lax.broadcasted_iota(jnp.int32, (ts, ts), 1)
    ltri = jnp.where(((ri // CHUNK) == (ci // CHUNK)) & (ci <= ri), 1.0, 0.0).astype(F32)
    acs = jnp.dot(ltri, dta, precision=HIGHEST, preferred_element_type=F32)
    acst_ref[...] = acs.T
    er = lax.broadcasted_iota(jnp.int32, (LANES, SSD_WIDTH), 0)
    ec = lax.broadcasted_iota(jnp.int32, (LANES, SSD_WIDTH), 1)
    expand = jnp.where((ec // SSD_HEAD_DIM) == er, 1.0, 0.0).astype(F32)
    dt_exp = jnp.dot(dt, expand, precision=HIGHEST, preferred_element_type=F32)
    acs_exp = jnp.dot(acs, expand, precision=HIGHEST, preferred_element_type=F32)
    acs_ref[...] = acs_exp
    eacs_ref[...] = jnp.exp(acs_exp)
    xdt_ref[...] = xc_ref[:, 0:SSD_WIDTH] * dt_exp
    bmt_ref[...] = xc_ref[:, SSD_WIDTH:SSD_WIDTH + SSD_GROUPS * SSD_STATE].T

    tr = lax.broadcasted_iota(jnp.int32, (CHUNK, CHUNK), 0)
    tc = lax.broadcasted_iota(jnp.int32, (CHUNK, CHUNK), 1)
    tril = tc <= tr
    c_off = SSD_WIDTH + SSD_GROUPS * SSD_STATE

    for c in range(nc):
        r0 = c * CHUNK
        rows = slice(r0, r0 + CHUNK)
        a_last = acs_ref[r0 + CHUNK - 1:r0 + CHUNK, :]
        xw = xdt_ref[rows, :] * jnp.exp(a_last - acs_ref[rows, :])
        cdec = jnp.exp(a_last)
        y_parts = []
        for g in range(SSD_GROUPS):
            cg = xc_ref[rows, c_off + g * SSD_STATE:c_off + (g + 1) * SSD_STATE].astype(BF16)
            bg = xc_ref[rows, SSD_WIDTH + g * SSD_STATE:SSD_WIDTH + (g + 1) * SSD_STATE].astype(BF16)
            cb = lax.dot_general(cg, bg, (((1,), (1,)), ((), ())), preferred_element_type=F32)
            hg = hst_ref[g]
            yoff = jnp.dot(cg, hg.astype(BF16), preferred_element_type=F32)
            st = _bdot(bmt_ref[g * SSD_STATE:(g + 1) * SSD_STATE, rows], xw[:, g * hw:(g + 1) * hw])
            hst_ref[g] = hg * cdec[:, g * hw:(g + 1) * hw] + st
            yds = []
            for hh in range(SSD_HEADS // SSD_GROUPS):
                h = g * (SSD_HEADS // SSD_GROUPS) + hh
                cols = slice(h * SSD_HEAD_DIM, (h + 1) * SSD_HEAD_DIM)
                seg = acs_ref[rows, cols] - acst_ref[h:h + 1, rows]
                dec = jnp.exp(jnp.where(tril, seg, -jnp.inf))
                yds.append(_bdot(cb * dec, xdt_ref[rows, cols]))
            y_parts.append(jnp.concatenate(yds, axis=1) + yoff * eacs_ref[rows, g * hw:(g + 1) * hw])
        y_ref[rows, :] = jnp.concatenate(y_parts, axis=1) + dexp_ref[...] * xc_ref[rows, 0:SSD_WIDTH]

    y = y_ref[...] * _silu(z_ref[...])
    y_ssd = y * lax.rsqrt(jnp.mean(y * y, axis=-1, keepdims=True) + NORM_EPS) * sng_ref[...]

    gext_ref[GLU_TAIL:GLU_TAIL + ts, :] = glu_ref[:, 0:CONF_WIDTH] * jax.nn.sigmoid(glu_ref[:, CONF_WIDTH:])
    span = ts + GLU_TAIL - SUBLANES
    for s in range(1, SUBLANES):
        gsh_ref[s - 1, 0:span, :] = gext_ref[pl.ds(s, span), :]
    u = dwb_ref[...]
    for k in range(CONF_CONV):
        off = GLU_TAIL - CONF_CONV + 1 + k
        s = off % SUBLANES
        rows = pl.ds(off - s, ts)
        u = u + dww_ref[k:k + 1, :] * (gext_ref[rows, :] if s == 0 else gsh_ref[s - 1, rows, :])
    gext_ref[0:GLU_TAIL, :] = gext_ref[ts:ts + GLU_TAIL, :]
    mu = jnp.mean(u, axis=-1, keepdims=True)
    uc = u - mu
    var = jnp.mean(uc * uc, axis=-1, keepdims=True)
    y_conf = _silu(uc * lax.rsqrt(var + NORM_EPS) * lng_ref[...] + lnb_ref[...])

    mix = (jnp.dot(y_ssd.astype(BF16), wout_ref[0:SSD_WIDTH, :], preferred_element_type=F32)
           + jnp.dot(y_conf.astype(BF16), wout_ref[SSD_WIDTH:, :], preferred_element_type=F32))
    h1 = x_ref[...] + mod_ref[0, 2:3, :] * mix
    h1_ref[...] = h1
    hn = h1 * lax.rsqrt(jnp.mean(h1 * h1, axis=-1, keepdims=True) + NORM_EPS) * n2g_ref[...]
    hn2_ref[...] = hn * (1.0 + mod_ref[0, 4:5, :]) + mod_ref[0, 3:4, :]


def _mixer_call(x2, z, xbc, glu, dt, mod3, cw, cb, dtb, alog, dexp, sng, dww, dwb, lng, lnb,
                wout, n2g, bsz, seq, ts):
    t, d = x2.shape
    per_b = seq // ts
    row = lambda b, j: (b * per_b + j, 0)
    const = lambda b, j: (0, 0)

    def full(a):
        return pl.BlockSpec(a.shape, const)

    return pl.pallas_call(
        functools.partial(_mixer_kernel, ts=ts),
        out_shape=(jax.ShapeDtypeStruct((t, d), F32), jax.ShapeDtypeStruct((t, d), F32)),
        grid=(bsz, per_b),
        in_specs=[pl.BlockSpec((ts, d), row),
                  pl.BlockSpec((ts, SSD_WIDTH), row),
                  pl.BlockSpec((ts, SSD_XBC), row),
                  pl.BlockSpec((ts, 2 * CONF_WIDTH), row),
                  pl.BlockSpec((ts, LANES), row),
                  pl.BlockSpec((1, 6, d), lambda b, j: (b, 0, 0)),
                  full(cw), full(cb), full(dtb), full(alog), full(dexp), full(sng),
                  full(dww), full(dwb), full(lng), full(lnb), full(wout), full(n2g)],
        out_specs=(pl.BlockSpec((ts, d), row), pl.BlockSpec((ts, d), row)),
        scratch_shapes=[pltpu.VMEM((ts + XBC_TAIL, SSD_XBC), F32),
                        pltpu.VMEM((ts + GLU_TAIL, CONF_WIDTH), F32),
                        pltpu.VMEM((SSD_GROUPS, SSD_STATE, SSD_WIDTH // SSD_GROUPS), F32),
                        pltpu.VMEM((ts, SSD_XBC), F32),
                        pltpu.VMEM((ts, SSD_WIDTH), F32),
                        pltpu.VMEM((ts, SSD_WIDTH), F32),
                        pltpu.VMEM((ts, SSD_WIDTH), F32),
                        pltpu.VMEM((LANES, ts), F32),
                        pltpu.VMEM((SSD_GROUPS * SSD_STATE, ts), F32),
                        pltpu.VMEM((ts, SSD_WIDTH), F32),
                        pltpu.VMEM((SUBLANES - 1, ts + GLU_TAIL - SUBLANES, CONF_WIDTH), F32)],
        compiler_params=pltpu.CompilerParams(
            dimension_semantics=("arbitrary", "arbitrary"), vmem_limit_bytes=VMEM_LIMIT),
        name="mixer",
    )(x2, z, xbc, glu, dt, mod3, cw, cb, dtb, alog, dexp, sng, dww, dwb, lng, lnb, wout, n2g)


_PAIR_COUNTS = tuple(PEER_TOPK // (a + 1) for a in range(PEER_TOPK))


def _topk_rows(s, k):
    n = s.shape[0]
    iota = lax.broadcasted_iota(jnp.int32, s.shape, 0).astype(F32)
    vals, rows = [], []
    for _ in range(k):
        m = jnp.max(s, axis=0, keepdims=True)
        am = jnp.min(jnp.where(s == m, iota, float(n)), axis=0, keepdims=True)
        vals.append(m)
        rows.append(am)
        s = jnp.where(iota == am, -jnp.inf, s)
    return vals, rows


def _route_kernel(hn_ref, wq_ref, keys_ref, idx_ref, gate_ref):
    q = jnp.dot(hn_ref[...].astype(BF16), wq_ref[...], preferred_element_type=F32)
    tm = q.shape[0]
    nt = (((1,), (1,)), ((), ()))
    n_cand = sum(_PAIR_COUNTS)
    n_pad = -n_cand % SUBLANES
    rows_out = []
    for h in range(PEER_HEADS):
        sv, si = [], []
        for i in range(2):
            col = (h * 2 + i) * PEER_D_HALF
            qh = q[:, col:col + PEER_D_HALF].astype(BF16)
            sc = lax.dot_general(keys_ref[h * 2 + i], qh, nt, preferred_element_type=F32)
            v, r = _topk_rows(sc, PEER_TOPK)
            sv.append(v)
            si.append(r)
        sv2 = jnp.concatenate(sv[1], axis=0)
        si2 = jnp.concatenate(si[1], axis=0)
        cand = jnp.concatenate([sv[0][a] + sv2[0:nb] for a, nb in enumerate(_PAIR_COUNTS)]
                               + [jnp.full((n_pad, tm), -jnp.inf, F32)], axis=0)
        cidx = jnp.concatenate([si[0][a] * float(PEER_N_KEYS) + si2[0:nb] for a, nb in enumerate(_PAIR_COUNTS)]
                               + [jnp.zeros((n_pad, tm), F32)], axis=0)
        iota = lax.broadcasted_iota(jnp.int32, cand.shape, 0).astype(F32)
        best, eidx = [], []
        for _ in range(PEER_TOPK):
            m = jnp.max(cand, axis=0, keepdims=True)
            p = jnp.min(jnp.where(cand == m, iota, float(cand.shape[0])), axis=0, keepdims=True)
            hit = iota == p
            best.append(m)
            eidx.append(jnp.max(jnp.where(hit, cidx, -1.0), axis=0, keepdims=True))
            cand = jnp.where(hit, -jnp.inf, cand)
        best = jnp.concatenate(best, axis=0)
        e = jnp.exp(best - best[0:1, :])
        gate_ref[h * PEER_TOPK:(h + 1) * PEER_TOPK, :] = e / jnp.sum(e, axis=0, keepdims=True)
        rows_out.append(jnp.concatenate(eidx, axis=0))
    expert = jnp.concatenate(rows_out, axis=0)
    idx_ref[...] = (expert.T * float(ROW_SUBLANES)).astype(jnp.int32)


def _route_call(hn2, wq, keys, tm):
    t, d = hn2.shape
    return pl.pallas_call(
        _route_kernel,
        out_shape=(jax.ShapeDtypeStruct((t, PEER_SLOTS), jnp.int32),
                   jax.ShapeDtypeStruct((PEER_SLOTS, t), F32)),
        grid=(t // tm,),
        in_specs=[pl.BlockSpec((tm, d), lambda i: (i, 0)),
                  pl.BlockSpec(wq.shape, lambda i: (0, 0)),
                  pl.BlockSpec(keys.shape, lambda i: (0, 0, 0))],
        out_specs=(pl.BlockSpec((tm, PEER_SLOTS), lambda i: (i, 0)),
                   pl.BlockSpec((PEER_SLOTS, tm), lambda i: (0, i))),
        compiler_params=pltpu.CompilerParams(vmem_limit_bytes=VMEM_LIMIT),
        name="route",
    )(hn2, wq, keys)


def _pack_kernel(t_ref, o_ref):
    half = t_ref.shape[1] // 2
    lo = pltpu.bitcast(t_ref[:, :half].astype(BF16).astype(F32), jnp.uint32)
    hi = pltpu.bitcast(t_ref[:, half:].astype(BF16).astype(F32), jnp.uint32)
    word = hi | (lo >> 16)
    rows = t_ref.shape[0]
    for s in range(ROW_SUBLANES):
        o_ref[pl.ds(s, rows, stride=ROW_SUBLANES), :] = word[:, s * LANES:(s + 1) * LANES]


def _pack_table(tbl):
    n, d = tbl.shape
    rows = 512
    return pl.pallas_call(
        _pack_kernel,
        out_shape=jax.ShapeDtypeStruct((n * ROW_SUBLANES, LANES), jnp.uint32),
        grid=(n // rows,),
        in_specs=[pl.BlockSpec((rows, d), lambda i: (i, 0))],
        out_specs=pl.BlockSpec((rows * ROW_SUBLANES, LANES), lambda i: (i, 0)),
        name="pack",
    )(tbl)


def _unpack_lo(w):
    return pltpu.bitcast(w << 16, F32)


def _unpack_hi(w):
    return pltpu.bitcast(w & jnp.uint32(0xFFFF0000), F32)


def _gather_rows(idx_ref, tbl_ref, slot_ref, t):
    for j in range(PEER_SLOTS):
        start = pl.multiple_of(idx_ref[t, j], ROW_SUBLANES)
        slot_ref[j * ROW_SUBLANES:(j + 1) * ROW_SUBLANES, :] = tbl_ref[pl.ds(start, ROW_SUBLANES), :]


def _peer_u_kernel(idx_ref, x_ref, gate_ref, tbl_ref, w_ref, slot_a, slot_b, prod_a, prod_b, x3_ref, *, tb):
    per_vreg = SUBLANES // ROW_SUBLANES
    lane = lax.broadcasted_iota(jnp.int32, (PEER_SLOTS, LANES), 1)
    for r in range(SUBLANES):
        x3_ref[:, r, :] = x_ref[:, r * LANES:(r + 1) * LANES]

    def products(slot_ref, prod_ref, t):
        xt = x3_ref[t]
        xlo = jnp.concatenate([xt[0:ROW_SUBLANES]] * per_vreg, axis=0)
        xhi = jnp.concatenate([xt[ROW_SUBLANES:]] * per_vreg, axis=0)
        words = slot_ref[...].reshape(PEER_SLOTS // per_vreg, SUBLANES, LANES)
        prod = _unpack_lo(words) * xlo[None] + _unpack_hi(words) * xhi[None]
        prod_ref[...] = prod.reshape(PEER_SLOTS * ROW_SUBLANES, LANES)

    def reduce_into(prod_ref, tl, acc):
        part = prod_ref[pl.ds(0, PEER_SLOTS, stride=ROW_SUBLANES), :]
        for r in range(1, ROW_SUBLANES):
            part = part + prod_ref[pl.ds(r, PEER_SLOTS, stride=ROW_SUBLANES), :]
        col = jnp.sum(part, axis=-1, keepdims=True)
        return jnp.where(lane == tl, col, acc)

    prod_b[...] = jnp.zeros(prod_b.shape, F32)
    per_trip = 2
    for blk in range(tb // LANES):
        base = blk * LANES
        _gather_rows(idx_ref, tbl_ref, slot_a, base)

        def body(i, acc):
            for k in range(0, per_trip, 2):
                tl = per_trip * i + k
                acc = reduce_into(prod_b, tl - 1, acc)
                products(slot_a, prod_a, base + tl)
                _gather_rows(idx_ref, tbl_ref, slot_b, base + tl + 1)
                acc = reduce_into(prod_a, tl, acc)
                products(slot_b, prod_b, base + tl + 1)
                _gather_rows(idx_ref, tbl_ref, slot_a, jnp.minimum(base + tl + 2, tb - 1))
            return acc

        act = lax.fori_loop(0, LANES // per_trip, body, jnp.zeros((PEER_SLOTS, LANES), F32))
        act = reduce_into(prod_b, LANES - 1, act)
        gelu = 0.5 * act * (1.0 + lax.erf(act * (1.0 / math.sqrt(2.0))))
        w_ref[blk * LANES:(blk + 1) * LANES, :] = (gate_ref[:, blk * LANES:(blk + 1) * LANES] * gelu).T


def _peer_v_kernel(idx_ref, w_ref, tbl_ref, o_ref, slot_a, slot_b, wrep_hi, wrep_lo, o3_ref, *, tb):
    cols = 2 * ROW_SUBLANES * PEER_SLOTS
    w = w_ref[...]
    hi = w.astype(BF16)
    lo = (w - hi.astype(F32)).astype(BF16)
    jr = lax.broadcasted_iota(jnp.int32, (PEER_SLOTS, cols), 0)
    jc = lax.broadcasted_iota(jnp.int32, (PEER_SLOTS, cols), 1)
    expand = jnp.where(jc // (2 * ROW_SUBLANES) == jr, 1.0, 0.0).astype(BF16)
    wrep_hi[...] = jnp.dot(hi, expand, preferred_element_type=F32)
    wrep_lo[...] = jnp.dot(lo, expand, preferred_element_type=F32)
    rr = lax.broadcasted_iota(jnp.int32, (SUBLANES, cols), 0)
    rc = lax.broadcasted_iota(jnp.int32, (SUBLANES, cols), 1)
    mask = (rc % (2 * ROW_SUBLANES)) == 2 * (rr % ROW_SUBLANES) + rr // ROW_SUBLANES

    def lhs_rows(t):
        return [jnp.where(mask, jnp.broadcast_to(ref[pl.ds(t, 1), :], (SUBLANES, cols)), 0.0)
                for ref in (wrep_hi, wrep_lo)]

    def gather_pair(slot_ref, t):
        for k in range(2):
            for j in range(PEER_SLOTS):
                start = pl.multiple_of(idx_ref[t + k, j], ROW_SUBLANES)
                slot_ref[j * ROW_SUBLANES:(j + 1) * ROW_SUBLANES, k * LANES:(k + 1) * LANES] = (
                    tbl_ref[pl.ds(start, ROW_SUBLANES), :])

    def store_token(t, val):
        o3_ref[t] = val

    def combine_pair(slot_ref, t):
        lhs = jnp.concatenate(lhs_rows(t) + lhs_rows(t + 1), axis=0).astype(BF16)
        res = jnp.dot(lhs, pltpu.bitcast(slot_ref[...], BF16), preferred_element_type=F32)
        store_token(t, res[0:SUBLANES, 0:LANES] + res[SUBLANES:2 * SUBLANES, 0:LANES])
        store_token(t + 1, res[2 * SUBLANES:3 * SUBLANES, LANES:] + res[3 * SUBLANES:, LANES:])

    pairs = slot_a.shape[0]
    half = 2 * pairs
    for q in range(pairs):
        gather_pair(slot_a.at[q], 2 * q)

    def body(i, carry):
        t0 = 2 * half * i
        for q in range(pairs):
            combine_pair(slot_a.at[q], t0 + 2 * q)
        for q in range(pairs):
            gather_pair(slot_b.at[q], t0 + half + 2 * q)
        for q in range(pairs):
            combine_pair(slot_b.at[q], t0 + half + 2 * q)
        for q in range(pairs):
            gather_pair(slot_a.at[q], jnp.minimum(t0 + 2 * half + 2 * q, tb - 2))
        return carry

    lax.fori_loop(0, tb // (2 * half), body, 0)
    for r in range(SUBLANES):
        o_ref[:, r * LANES:(r + 1) * LANES] = o3_ref[:, r, :]


def _table_spec(tbl):
    return pl.BlockSpec(tbl.shape, lambda i: (0, 0), pipeline_mode=pl.Buffered(1))


def _peer_u_call(idx_t, x2, gate_t, tbl, tb):
    t, d = x2.shape
    return pl.pallas_call(
        functools.partial(_peer_u_kernel, tb=tb),
        out_shape=jax.ShapeDtypeStruct((t, PEER_SLOTS), F32),
        grid=(t // tb,),
        in_specs=[pl.BlockSpec((tb, PEER_SLOTS), lambda i: (i, 0), memory_space=pltpu.SMEM),
                  pl.BlockSpec((tb, d), lambda i: (i, 0)),
                  pl.BlockSpec((PEER_SLOTS, tb), lambda i: (0, i)),
                  _table_spec(tbl)],
        out_specs=pl.BlockSpec((tb, PEER_SLOTS), lambda i: (i, 0)),
        scratch_shapes=[pltpu.VMEM((PEER_SLOTS * ROW_SUBLANES, LANES), jnp.uint32),
                        pltpu.VMEM((PEER_SLOTS * ROW_SUBLANES, LANES), jnp.uint32),
                        pltpu.VMEM((PEER_SLOTS * ROW_SUBLANES, LANES), F32),
                        pltpu.VMEM((PEER_SLOTS * ROW_SUBLANES, LANES), F32),
                        pltpu.VMEM((tb, SUBLANES, LANES), F32)],
        compiler_params=pltpu.CompilerParams(vmem_limit_bytes=VMEM_LIMIT),
        name="peer_u",
    )(idx_t, x2, gate_t, tbl)


def _peer_v_call(idx_t, w_t, tbl, tb):
    t = idx_t.shape[0]
    return pl.pallas_call(
        functools.partial(_peer_v_kernel, tb=tb),
        out_shape=jax.ShapeDtypeStruct((t, D_MODEL), F32),
        grid=(t // tb,),
        in_specs=[pl.BlockSpec((tb, PEER_SLOTS), lambda i: (i, 0), memory_space=pltpu.SMEM),
                  pl.BlockSpec((tb, PEER_SLOTS), lambda i: (i, 0)),
                  _table_spec(tbl)],
        out_specs=pl.BlockSpec((tb, D_MODEL), lambda i: (i, 0)),
        scratch_shapes=[pltpu.VMEM((2, PEER_SLOTS * ROW_SUBLANES, 2 * LANES), jnp.uint32),
                        pltpu.VMEM((2, PEER_SLOTS * ROW_SUBLANES, 2 * LANES), jnp.uint32),
                        pltpu.VMEM((tb, 2 * ROW_SUBLANES * PEER_SLOTS), F32),
                        pltpu.VMEM((tb, 2 * ROW_SUBLANES * PEER_SLOTS), F32),
                        pltpu.VMEM((tb, SUBLANES, LANES), F32)],
        compiler_params=pltpu.CompilerParams(vmem_limit_bytes=VMEM_LIMIT),
        name="peer_v",
    )(idx_t, w_t, tbl)


def _final_kernel(h1_ref, p_ref, mod_ref, g_ref, o_ref):
    h = h1_ref[...] + mod_ref[0, 5:6, :] * p_ref[...]
    o_ref[...] = h * lax.rsqrt(jnp.mean(h * h, axis=-1, keepdims=True) + NORM_EPS) * g_ref[...]


def _final_call(h1, peer, mod3, g, seq, tm):
    t, d = h1.shape
    per_b = seq // tm
    row = lambda i: (i, 0)
    return pl.pallas_call(
        _final_kernel,
        out_shape=jax.ShapeDtypeStruct((t, d), F32),
        grid=(t // tm,),
        in_specs=[pl.BlockSpec((tm, d), row), pl.BlockSpec((tm, d), row),
                  pl.BlockSpec((1, 6, d), lambda i: (i // per_b, 0, 0)),
                  pl.BlockSpec((1, d), lambda i: (0, 0))],
        out_specs=pl.BlockSpec((tm, d), row),
        name="final",
    )(h1, peer, mod3, g.reshape(1, d))


def _pad_lanes(v):
    return jnp.pad(v.reshape(1, -1), ((0, 0), (0, LANES - v.shape[-1])))


def kernel(x, c, ada_w, ada_b, norm1_g, w_in, ssd_conv_w, ssd_conv_b, ssd_dt_bias, ssd_a_log, ssd_d, ssd_norm_g, conf_dw_w, conf_dw_b, conf_ln_g, conf_ln_b, w_out, norm2_g, peer_w_query, peer_sub_keys, peer_u, peer_v, final_norm_g):
    bsz, seq, d = x.shape
    assert d == D_MODEL and ada_w.shape[0] == 1
    t = bsz * seq
    tm = min(512, seq)
    ts = min(256, seq)
    tr = min(256, seq)
    tb = min(512, seq)
    x2 = x.reshape(t, d)

    mod3 = _mod_call(c, ada_w[0], ada_b[0]).reshape(bsz, 6, d)

    wi = w_in[0]
    o1 = SSD_WIDTH
    o2 = o1 + SSD_XBC
    o3 = o2 + SSD_HEADS
    wz = wi[:, :o1].astype(BF16)
    wx = wi[:, o1:o2].astype(BF16)
    wd = jnp.pad(wi[:, o2:o3], ((0, 0), (0, LANES - SSD_HEADS))).astype(BF16)
    wg = wi[:, o3:].astype(BF16)
    z, xbc, glu, dt = _inproj_call(x2, mod3, norm1_g[0], wz, wx, wg, wd, seq, tm)

    h1, hn2 = _mixer_call(
        x2, z, xbc, glu, dt, mod3,
        ssd_conv_w[0], ssd_conv_b[0].reshape(1, -1), _pad_lanes(ssd_dt_bias[0]), _pad_lanes(ssd_a_log[0]),
        jnp.repeat(ssd_d[0], SSD_HEAD_DIM).reshape(1, -1), ssd_norm_g[0].reshape(1, -1),
        conf_dw_w[0], conf_dw_b[0].reshape(1, -1), conf_ln_g[0].reshape(1, -1), conf_ln_b[0].reshape(1, -1),
        w_out[0].astype(BF16), norm2_g[0].reshape(1, -1), bsz, seq, ts)

    keys = peer_sub_keys[0].reshape(PEER_HEADS * 2, PEER_N_KEYS, PEER_D_HALF).astype(BF16)
    idx_t, gate_t = _route_call(hn2, peer_w_query[0].astype(BF16), keys, tr)

    w_t = _peer_u_call(idx_t, hn2, gate_t, _pack_table(peer_u[0]), tb)
    peer = _peer_v_call(idx_t, w_t, _pack_table(peer_v[0]), tb)

    out = _final_call(h1, peer, mod3, final_norm_g, seq, tm)
    return out.reshape(bsz, seq, d)
```

```python
import functools
import math

import jax
import jax.numpy as jnp
from jax import lax
from jax.experimental import pallas as pl
from jax.experimental.pallas import tpu as pltpu

F32 = jnp.float32
BF16 = jnp.bfloat16
HIGHEST = lax.Precision.HIGHEST

D_MODEL = 1024
CHUNK = 64
SSD_WIDTH = 512
SSD_HEADS = 8
SSD_HEAD_DIM = 64
SSD_GROUPS = 2
SSD_STATE = 128
SSD_CONV = 4
SSD_XBC = 1024
CONF_WIDTH = 512
CONF_CONV = 31
PEER_HEADS = 8
PEER_N_KEYS = 128
PEER_D_HALF = 128
PEER_TOPK = 16
PEER_SLOTS = PEER_HEADS * PEER_TOPK
NORM_EPS = 1e-6

LANES = 128
SUBLANES = 8
ROW_SUBLANES = D_MODEL // 2 // LANES
VMEM_LIMIT = 56 * 1024 * 1024

XBC_TAIL = 8
GLU_TAIL = 32


def _silu(v):
    return v * jax.nn.sigmoid(v)


def _softplus(v):
    return jnp.maximum(v, 0.0) + jnp.log(1.0 + jnp.exp(-jnp.abs(v)))


def _bdot(a, b):
    return jnp.dot(a.astype(BF16), b.astype(BF16), preferred_element_type=F32)


def _mod_kernel(c_ref, w_ref, b_ref, o_ref):
    cond = _silu(c_ref[...])
    o_ref[...] = jnp.dot(cond, w_ref[...], precision=HIGHEST, preferred_element_type=F32) + b_ref[...]


def _mod_call(c, ada_w, ada_b):
    bsz, d = c.shape
    n = ada_w.shape[1]
    return pl.pallas_call(
        _mod_kernel,
        out_shape=jax.ShapeDtypeStruct((bsz, n), F32),
        grid=(n // d,),
        in_specs=[pl.BlockSpec((bsz, d), lambda i: (0, 0)),
                  pl.BlockSpec((d, d), lambda i: (0, i)),
                  pl.BlockSpec((1, d), lambda i: (0, i))],
        out_specs=pl.BlockSpec((bsz, d), lambda i: (0, i)),
        name="mod",
    )(c, ada_w, ada_b.reshape(1, n))


def _inproj_kernel(x_ref, mod_ref, g_ref, wz_ref, wx_ref, wg_ref, wd_ref,
                   z_ref, xbc_ref, glu_ref, dt_ref):
    x = x_ref[...]
    ms = jnp.mean(x * x, axis=-1, keepdims=True)
    y = x * lax.rsqrt(ms + NORM_EPS) * g_ref[...]
    sh = mod_ref[0, 0:1, :]
    sc = mod_ref[0, 1:2, :]
    hn = (y * (1.0 + sc) + sh).astype(BF16)
    z_ref[...] = jnp.dot(hn, wz_ref[...], preferred_element_type=F32)
    xbc_ref[...] = jnp.dot(hn, wx_ref[...], preferred_element_type=F32)
    glu_ref[...] = jnp.dot(hn, wg_ref[...], preferred_element_type=F32)
    dt_ref[...] = jnp.dot(hn, wd_ref[...], preferred_element_type=F32)


def _inproj_call(x2, mod3, norm1_g, wz, wx, wg, wd, seq, tm):
    t, d = x2.shape
    per_b = seq // tm
    const = lambda i: (0, 0)
    row = lambda i: (i, 0)
    return pl.pallas_call(
        _inproj_kernel,
        out_shape=(jax.ShapeDtypeStruct((t, SSD_WIDTH), F32),
                   jax.ShapeDtypeStruct((t, SSD_XBC), F32),
                   jax.ShapeDtypeStruct((t, 2 * CONF_WIDTH), F32),
                   jax.ShapeDtypeStruct((t, LANES), F32)),
        grid=(t // tm,),
        in_specs=[pl.BlockSpec((tm, d), row),
                  pl.BlockSpec((1, 6, d), lambda i: (i // per_b, 0, 0)),
                  pl.BlockSpec((1, d), const),
                  pl.BlockSpec(wz.shape, const),
                  pl.BlockSpec(wx.shape, const),
                  pl.BlockSpec(wg.shape, const),
                  pl.BlockSpec(wd.shape, const)],
        out_specs=(pl.BlockSpec((tm, SSD_WIDTH), row),
                   pl.BlockSpec((tm, SSD_XBC), row),
                   pl.BlockSpec((tm, 2 * CONF_WIDTH), row),
                   pl.BlockSpec((tm, LANES), row)),
        compiler_params=pltpu.CompilerParams(vmem_limit_bytes=VMEM_LIMIT),
        name="inproj",
    )(x2, mod3, norm1_g.reshape(1, d), wz, wx, wg, wd)


def _mixer_kernel(x_ref, z_ref, xbc_ref, glu_ref, dt_ref, mod_ref,
                  cw_ref, cb_ref, dtb_ref, alog_ref, dexp_ref, sng_ref,
                  dww_ref, dwb_ref, lng_ref, lnb_ref, wout_ref, n2g_ref,
                  h1_ref, hn2_ref,
                  xext_ref, gext_ref, hst_ref, xc_ref, xdt_ref, acs_ref, eacs_ref,
                  acst_ref, bmt_ref, y_ref, gsh_ref, *, ts):
    nc = ts // CHUNK
    hw = SSD_WIDTH // SSD_GROUPS

    @pl.when(pl.program_id(1) == 0)
    def _():
        xext_ref[0:XBC_TAIL, :] = jnp.zeros((XBC_TAIL, SSD_XBC), F32)
        gext_ref[0:GLU_TAIL, :] = jnp.zeros((GLU_TAIL, CONF_WIDTH), F32)
        hst_ref[...] = jnp.zeros(hst_ref.shape, F32)

    xext_ref[XBC_TAIL:XBC_TAIL + ts, :] = xbc_ref[...]
    acc = cb_ref[...] + cw_ref[0:1, :] * xext_ref[pl.ds(XBC_TAIL - SSD_CONV + 1, ts), :]
    for k in range(1, SSD_CONV):
        acc = acc + cw_ref[k:k + 1, :] * xext_ref[pl.ds(XBC_TAIL - SSD_CONV + 1 + k, ts), :]
    xext_ref[0:XBC_TAIL, :] = xext_ref[ts:ts + XBC_TAIL, :]
    xc_ref[...] = _silu(acc)

    dt = _softplus(dt_ref[...] + dtb_ref[...])
    dta = dt * (-jnp.exp(alog_ref[...]))
    ri = lax.broadcasted_iota(jnp.int32, (ts, ts), 0)
    ci = lax.broadcasted_iota(jnp.int32, (ts, ts), 1)
    ltri = jnp.where(((ri // CHUNK) == (ci // CHUNK)) & (ci <= ri), 1.0, 0.0).astype(F32)
    acs = jnp.dot(ltri, dta, precision=HIGHEST, preferred_element_type=F32)
    acst_ref[...] = acs.T
    er = lax.broadcasted_iota(jnp.int32, (LANES, SSD_WIDTH), 0)
    ec = lax.broadcasted_iota(jnp.int32, (LANES, SSD_WIDTH), 1)
    expand = jnp.where((ec // SSD_HEAD_DIM) == er, 1.0, 0.0).astype(F32)
    dt_exp = jnp.dot(dt, expand, precision=HIGHEST, preferred_element_type=F32)
    acs_exp = jnp.dot(acs, expand, precision=HIGHEST, preferred_element_type=F32)
    acs_ref[...] = acs_exp
    eacs_ref[...] = jnp.exp(acs_exp)
    xdt_ref[...] = xc_ref[:, 0:SSD_WIDTH] * dt_exp
    bmt_ref[...] = xc_ref[:, SSD_WIDTH:SSD_WIDTH + SSD_GROUPS * SSD_STATE].T

    tr = lax.broadcasted_iota(jnp.int32, (CHUNK, CHUNK), 0)
    tc = lax.broadcasted_iota(jnp.int32, (CHUNK, CHUNK), 1)
    tril = tc <= tr
    c_off = SSD_WIDTH + SSD_GROUPS * SSD_STATE

    for c in range(nc):
        r0 = c * CHUNK
        rows = slice(r0, r0 + CHUNK)
        a_last = acs_ref[r0 + CHUNK - 1:r0 + CHUNK, :]
        xw = xdt_ref[rows, :] * jnp.exp(a_last - acs_ref[rows, :])
        cdec = jnp.exp(a_last)
        y_parts = []
        for g in range(SSD_GROUPS):
            cg = xc_ref[rows, c_off + g * SSD_STATE:c_off + (g + 1) * SSD_STATE].astype(BF16)
            bg = xc_ref[rows, SSD_WIDTH + g * SSD_STATE:SSD_WIDTH + (g + 1) * SSD_STATE].astype(BF16)
            cb = lax.dot_general(cg, bg, (((1,), (1,)), ((), ())), preferred_element_type=F32)
            hg = hst_ref[g]
            yoff = jnp.dot(cg, hg.astype(BF16), preferred_element_type=F32)
            st = _bdot(bmt_ref[g * SSD_STATE:(g + 1) * SSD_STATE, rows], xw[:, g * hw:(g + 1) * hw])
            hst_ref[g] = hg * cdec[:, g * hw:(g + 1) * hw] + st
            yds = []
            for hh in range(SSD_HEADS // SSD_GROUPS):
                h = g * (SSD_HEADS // SSD_GROUPS) + hh
                cols = slice(h * SSD_HEAD_DIM, (h + 1) * SSD_HEAD_DIM)
                seg = acs_ref[rows, cols] - acst_ref[h:h + 1, rows]
                dec = jnp.exp(jnp.where(tril, seg, -jnp.inf))
                yds.append(_bdot(cb * dec, xdt_ref[rows, cols]))
            y_parts.append(jnp.concatenate(yds, axis=1) + yoff * eacs_ref[rows, g * hw:(g + 1) * hw])
        y_ref[rows, :] = jnp.concatenate(y_parts, axis=1) + dexp_ref[...] * xc_ref[rows, 0:SSD_WIDTH]

    y = y_ref[...] * _silu(z_ref[...])
    y_ssd = y * lax.rsqrt(jnp.mean(y * y, axis=-1, keepdims=True) + NORM_EPS) * sng_ref[...]

    gext_ref[GLU_TAIL:GLU_TAIL + ts, :] = glu_ref[:, 0:CONF_WIDTH] * jax.nn.sigmoid(glu_ref[:, CONF_WIDTH:])
    span = ts + GLU_TAIL - SUBLANES
    for s in range(1, SUBLANES):
        gsh_ref[s - 1, 0:span, :] = gext_ref[pl.ds(s, span), :]
    u = dwb_ref[...]
    for k in range(CONF_CONV):
        off = GLU_TAIL - CONF_CONV + 1 + k
        s = off % SUBLANES
        rows = pl.ds(off - s, ts)
        u = u + dww_ref[k:k + 1, :] * (gext_ref[rows, :] if s == 0 else gsh_ref[s - 1, rows, :])
    gext_ref[0:GLU_TAIL, :] = gext_ref[ts:ts + GLU_TAIL, :]
    mu = jnp.mean(u, axis=-1, keepdims=True)
    uc = u - mu
    var = jnp.mean(uc * uc, axis=-1, keepdims=True)
    y_conf = _silu(uc * lax.rsqrt(var + NORM_EPS) * lng_ref[...] + lnb_ref[...])

    mix = (jnp.dot(y_ssd.astype(BF16), wout_ref[0:SSD_WIDTH, :], preferred_element_type=F32)
           + jnp.dot(y_conf.astype(BF16), wout_ref[SSD_WIDTH:, :], preferred_element_type=F32))
    h1 = x_ref[...] + mod_ref[0, 2:3, :] * mix
    h1_ref[...] = h1
    hn = h1 * lax.rsqrt(jnp.mean(h1 * h1, axis=-1, keepdims=True) + NORM_EPS) * n2g_ref[...]
    hn2_ref[...] = hn * (1.0 + mod_ref[0, 4:5, :]) + mod_ref[0, 3:4, :]


def _mixer_call(x2, z, xbc, glu, dt, mod3, cw, cb, dtb, alog, dexp, sng, dww, dwb, lng, lnb,
                wout, n2g, bsz, seq, ts):
    t, d = x2.shape
    per_b = seq // ts
    row = lambda b, j: (b * per_b + j, 0)
    const = lambda b, j: (0, 0)

    def full(a):
        return pl.BlockSpec(a.shape, const)

    return pl.pallas_call(
        functools.partial(_mixer_kernel, ts=ts),
        out_shape=(jax.ShapeDtypeStruct((t, d), F32), jax.ShapeDtypeStruct((t, d), F32)),
        grid=(bsz, per_b),
        in_specs=[pl.BlockSpec((ts, d), row),
                  pl.BlockSpec((ts, SSD_WIDTH), row),
                  pl.BlockSpec((ts, SSD_XBC), row),
                  pl.BlockSpec((ts, 2 * CONF_WIDTH), row),
                  pl.BlockSpec((ts, LANES), row),
                  pl.BlockSpec((1, 6, d), lambda b, j: (b, 0, 0)),
                  full(cw), full(cb), full(dtb), full(alog), full(dexp), full(sng),
                  full(dww), full(dwb), full(lng), full(lnb), full(wout), full(n2g)],
        out_specs=(pl.BlockSpec((ts, d), row), pl.BlockSpec((ts, d), row)),
        scratch_shapes=[pltpu.VMEM((ts + XBC_TAIL, SSD_XBC), F32),
                        pltpu.VMEM((ts + GLU_TAIL, CONF_WIDTH), F32),
                        pltpu.VMEM((SSD_GROUPS, SSD_STATE, SSD_WIDTH // SSD_GROUPS), F32),
                        pltpu.VMEM((ts, SSD_XBC), F32),
                        pltpu.VMEM((ts, SSD_WIDTH), F32),
                        pltpu.VMEM((ts, SSD_WIDTH), F32),
                        pltpu.VMEM((ts, SSD_WIDTH), F32),
                        pltpu.VMEM((LANES, ts), F32),
                        pltpu.VMEM((SSD_GROUPS * SSD_STATE, ts), F32),
                        pltpu.VMEM((ts, SSD_WIDTH), F32),
                        pltpu.VMEM((SUBLANES - 1, ts + GLU_TAIL - SUBLANES, CONF_WIDTH), F32)],
        compiler_params=pltpu.CompilerParams(
            dimension_semantics=("arbitrary", "arbitrary"), vmem_limit_bytes=VMEM_LIMIT),
        name="mixer",
    )(x2, z, xbc, glu, dt, mod3, cw, cb, dtb, alog, dexp, sng, dww, dwb, lng, lnb, wout, n2g)


_PAIR_COUNTS = tuple(PEER_TOPK // (a + 1) for a in range(PEER_TOPK))


def _topk_rows(s, k):
    n = s.shape[0]
    iota = lax.broadcasted_iota(jnp.int32, s.shape, 0).astype(F32)
    vals, rows = [], []
    for _ in range(k):
        m = jnp.max(s, axis=0, keepdims=True)
        am = jnp.min(jnp.where(s == m, iota, float(n)), axis=0, keepdims=True)
        vals.append(m)
        rows.append(am)
        s = jnp.where(iota == am, -jnp.inf, s)
    return vals, rows


def _route_kernel(hn_ref, wq_ref, keys_ref, idx_ref, gate_ref):
    q = jnp.dot(hn_ref[...].astype(BF16), wq_ref[...], preferred_element_type=F32)
    tm = q.shape[0]
    nt = (((1,), (1,)), ((), ()))
    n_cand = sum(_PAIR_COUNTS)
    n_pad = -n_cand % SUBLANES
    rows_out = []
    for h in range(PEER_HEADS):
        sv, si = [], []
        for i in range(2):
            col = (h * 2 + i) * PEER_D_HALF
            qh = q[:, col:col + PEER_D_HALF].astype(BF16)
            sc = lax.dot_general(keys_ref[h * 2 + i], qh, nt, preferred_element_type=F32)
            v, r = _topk_rows(sc, PEER_TOPK)
            sv.append(v)
            si.append(r)
        sv2 = jnp.concatenate(sv[1], axis=0)
        si2 = jnp.concatenate(si[1], axis=0)
        cand = jnp.concatenate([sv[0][a] + sv2[0:nb] for a, nb in enumerate(_PAIR_COUNTS)]
                               + [jnp.full((n_pad, tm), -jnp.inf, F32)], axis=0)
        cidx = jnp.concatenate([si[0][a] * float(PEER_N_KEYS) + si2[0:nb] for a, nb in enumerate(_PAIR_COUNTS)]
                               + [jnp.zeros((n_pad, tm), F32)], axis=0)
        iota = lax.broadcasted_iota(jnp.int32, cand.shape, 0).astype(F32)
        best, eidx = [], []
        for _ in range(PEER_TOPK):
            m = jnp.max(cand, axis=0, keepdims=True)
            p = jnp.min(jnp.where(cand == m, iota, float(cand.shape[0])), axis=0, keepdims=True)
            hit = iota == p
            best.append(m)
            eidx.append(jnp.max(jnp.where(hit, cidx, -1.0), axis=0, keepdims=True))
            cand = jnp.where(hit, -jnp.inf, cand)
        best = jnp.concatenate(best, axis=0)
        e = jnp.exp(best - best[0:1, :])
        gate_ref[h * PEER_TOPK:(h + 1) * PEER_TOPK, :] = e / jnp.sum(e, axis=0, keepdims=True)
        rows_out.append(jnp.concatenate(eidx, axis=0))
    expert = jnp.concatenate(rows_out, axis=0)
    idx_ref[...] = (expert.T * float(ROW_SUBLANES)).astype(jnp.int32)


def _route_call(hn2, wq, keys, tm):
    t, d = hn2.shape
    return pl.pallas_call(
        _route_kernel,
        out_shape=(jax.ShapeDtypeStruct((t, PEER_SLOTS), jnp.int32),
                   jax.ShapeDtypeStruct((PEER_SLOTS, t), F32)),
        grid=(t // tm,),
        in_specs=[pl.BlockSpec((tm, d), lambda i: (i, 0)),
                  pl.BlockSpec(wq.shape, lambda i: (0, 0)),
                  pl.BlockSpec(keys.shape, lambda i: (0, 0, 0))],
        out_specs=(pl.BlockSpec((tm, PEER_SLOTS), lambda i: (i, 0)),
                   pl.BlockSpec((PEER_SLOTS, tm), lambda i: (0, i))),
        compiler_params=pltpu.CompilerParams(vmem_limit_bytes=VMEM_LIMIT),
        name="route",
    )(hn2, wq, keys)


def _pack_kernel(t_ref, o_ref):
    half = t_ref.shape[1] // 2
    lo = pltpu.bitcast(t_ref[:, :half].astype(BF16).astype(F32), jnp.uint32)
    hi = pltpu.bitcast(t_ref[:, half:].astype(BF16).astype(F32), jnp.uint32)
    word = hi | (lo >> 16)
    rows = t_ref.shape[0]
    for s in range(ROW_SUBLANES):
        o_ref[pl.ds(s, rows, stride=ROW_SUBLANES), :] = word[:, s * LANES:(s + 1) * LANES]


def _pack_table(tbl):
    n, d = tbl.shape
    rows = 512
    return pl.pallas_call(
        _pack_kernel,
        out_shape=jax.ShapeDtypeStruct((n * ROW_SUBLANES, LANES), jnp.uint32),
        grid=(n // rows,),
        in_specs=[pl.BlockSpec((rows, d), lambda i: (i, 0))],
        out_specs=pl.BlockSpec((rows * ROW_SUBLANES, LANES), lambda i: (i, 0)),
        name="pack",
    )(tbl)


def _unpack_lo(w):
    return pltpu.bitcast(w << 16, F32)


def _unpack_hi(w):
    return pltpu.bitcast(w & jnp.uint32(0xFFFF0000), F32)


def _gather_rows(idx_ref, tbl_ref, slot_ref, t):
    for j in range(PEER_SLOTS):
        start = pl.multiple_of(idx_ref[t, j], ROW_SUBLANES)
        slot_ref[j * ROW_SUBLANES:(j + 1) * ROW_SUBLANES, :] = tbl_ref[pl.ds(start, ROW_SUBLANES), :]


def _peer_u_kernel(idx_ref, x_ref, gate_ref, tbl_ref, w_ref, slot_a, slot_b, prod_a, prod_b, x3_ref, *, tb):
    per_vreg = SUBLANES // ROW_SUBLANES
    lane = lax.broadcasted_iota(jnp.int32, (PEER_SLOTS, LANES), 1)
    for r in range(SUBLANES):
        x3_ref[:, r, :] = x_ref[:, r * LANES:(r + 1) * LANES]

    def products(slot_ref, prod_ref, t):
        xt = x3_ref[t]
        xlo = jnp.concatenate([xt[0:ROW_SUBLANES]] * per_vreg, axis=0)
        xhi = jnp.concatenate([xt[ROW_SUBLANES:]] * per_vreg, axis=0)
        words = slot_ref[...].reshape(PEER_SLOTS // per_vreg, SUBLANES, LANES)
        prod = _unpack_lo(words) * xlo[None] + _unpack_hi(words) * xhi[None]
        prod_ref[...] = prod.reshape(PEER_SLOTS * ROW_SUBLANES, LANES)

    def reduce_into(prod_ref, tl, acc):
        part = prod_ref[pl.ds(0, PEER_SLOTS, stride=ROW_SUBLANES), :]
        for r in range(1, ROW_SUBLANES):
            part = part + prod_ref[pl.ds(r, PEER_SLOTS, stride=ROW_SUBLANES), :]
        col = jnp.sum(part, axis=-1, keepdims=True)
        return jnp.where(lane == tl, col, acc)

    prod_a[...] = jnp.zeros(prod_a.shape, F32)
    prod_b[...] = jnp.zeros(prod_b.shape, F32)
    for blk in range(tb // LANES):
        base = blk * LANES
        _gather_rows(idx_ref, tbl_ref, slot_a, base)
        _gather_rows(idx_ref, tbl_ref, slot_b, base + 1)

        def body(i, acc):
            tl = 2 * i
            acc = reduce_into(prod_a, tl - 2, acc)
            acc = reduce_into(prod_b, tl - 1, acc)
            products(slot_a, prod_a, base + tl)
            _gather_rows(idx_ref, tbl_ref, slot_a, jnp.minimum(base + tl + 2, tb - 1))
            products(slot_b, prod_b, base + tl + 1)
            _gather_rows(idx_ref, tbl_ref, slot_b, jnp.minimum(base + tl + 3, tb - 1))
            return acc

        act = lax.fori_loop(0, LANES // 2, body, jnp.zeros((PEER_SLOTS, LANES), F32))
        act = reduce_into(prod_a, LANES - 2, act)
        act = reduce_into(prod_b, LANES - 1, act)
        gelu = 0.5 * act * (1.0 + lax.erf(act * (1.0 / math.sqrt(2.0))))
        w_ref[blk * LANES:(blk + 1) * LANES, :] = (gate_ref[:, blk * LANES:(blk + 1) * LANES] * gelu).T


def _peer_v_kernel(idx_ref, w_ref, tbl_ref, o_ref, slot_a, slot_b, wrep_hi, wrep_lo, o3_ref, *, tb):
    cols = 2 * ROW_SUBLANES * PEER_SLOTS
    w = w_ref[...]
    hi = w.astype(BF16)
    lo = (w - hi.astype(F32)).astype(BF16)
    jr = lax.broadcasted_iota(jnp.int32, (PEER_SLOTS, cols), 0)
    jc = lax.broadcasted_iota(jnp.int32, (PEER_SLOTS, cols), 1)
    expand = jnp.where(jc // (2 * ROW_SUBLANES) == jr, 1.0, 0.0).astype(BF16)
    wrep_hi[...] = jnp.dot(hi, expand, preferred_element_type=F32)
    wrep_lo[...] = jnp.dot(lo, expand, preferred_element_type=F32)
    rr = lax.broadcasted_iota(jnp.int32, (SUBLANES, cols), 0)
    rc = lax.broadcasted_iota(jnp.int32, (SUBLANES, cols), 1)
    mask = (rc % (2 * ROW_SUBLANES)) == 2 * (rr % ROW_SUBLANES) + rr // ROW_SUBLANES

    def lhs_rows(t):
        return [jnp.where(mask, jnp.broadcast_to(ref[pl.ds(t, 1), :], (SUBLANES, cols)), 0.0)
                for ref in (wrep_hi, wrep_lo)]

    def gather_pair(slot_ref, t):
        for k in range(2):
            for j in range(PEER_SLOTS):
                start = pl.multiple_of(idx_ref[t + k, j], ROW_SUBLANES)
                slot_ref[j * ROW_SUBLANES:(j + 1) * ROW_SUBLANES, k * LANES:(k + 1) * LANES] = (
                    tbl_ref[pl.ds(start, ROW_SUBLANES), :])

    def store_token(t, val):
        o3_ref[t] = val

    def combine_pair(slot_ref, t):
        lhs = jnp.concatenate(lhs_rows(t) + lhs_rows(t + 1), axis=0).astype(BF16)
        res = jnp.dot(lhs, pltpu.bitcast(slot_ref[...], BF16), preferred_element_type=F32)
        store_token(t, res[0:SUBLANES, 0:LANES] + res[SUBLANES:2 * SUBLANES, 0:LANES])
        store_token(t + 1, res[2 * SUBLANES:3 * SUBLANES, LANES:] + res[3 * SUBLANES:, LANES:])

    pairs = slot_a.shape[0]
    half = 2 * pairs
    for q in range(pairs):
        gather_pair(slot_a.at[q], 2 * q)
        gather_pair(slot_b.at[q], half + 2 * q)

    def body(i, carry):
        t0 = 2 * half * i
        for q in range(pairs):
            combine_pair(slot_a.at[q], t0 + 2 * q)
        for q in range(pairs):
            gather_pair(slot_a.at[q], jnp.minimum(t0 + 2 * half + 2 * q, tb - 2))
        for q in range(pairs):
            combine_pair(slot_b.at[q], t0 + half + 2 * q)
        for q in range(pairs):
            gather_pair(slot_b.at[q], jnp.minimum(t0 + 3 * half + 2 * q, tb - 2))
        return carry

    lax.fori_loop(0, tb // (2 * half), body, 0)
    for r in range(SUBLANES):
        o_ref[:, r * LANES:(r + 1) * LANES] = o3_ref[:, r, :]


def _table_spec(tbl):
    return pl.BlockSpec(tbl.shape, lambda i: (0, 0), pipeline_mode=pl.Buffered(1))


def _peer_u_call(idx_t, x2, gate_t, tbl, tb):
    t, d = x2.shape
    return pl.pallas_call(
        functools.partial(_peer_u_kernel, tb=tb),
        out_shape=jax.ShapeDtypeStruct((t, PEER_SLOTS), F32),
        grid=(t // tb,),
        in_specs=[pl.BlockSpec((tb, PEER_SLOTS), lambda i: (i, 0), memory_space=pltpu.SMEM),
                  pl.BlockSpec((tb, d), lambda i: (i, 0)),
                  pl.BlockSpec((PEER_SLOTS, tb), lambda i: (0, i)),
                  _table_spec(tbl)],
        out_specs=pl.BlockSpec((tb, PEER_SLOTS), lambda i: (i, 0)),
        scratch_shapes=[pltpu.VMEM((PEER_SLOTS * ROW_SUBLANES, LANES), jnp.uint32),
                        pltpu.VMEM((PEER_SLOTS * ROW_SUBLANES, LANES), jnp.uint32),
                        pltpu.VMEM((PEER_SLOTS * ROW_SUBLANES, LANES), F32),
                        pltpu.VMEM((PEER_SLOTS * ROW_SUBLANES, LANES), F32),
                        pltpu.VMEM((tb, SUBLANES, LANES), F32)],
        compiler_params=pltpu.CompilerParams(vmem_limit_bytes=VMEM_LIMIT),
        name="peer_u",
    )(idx_t, x2, gate_t, tbl)


def _peer_v_call(idx_t, w_t, tbl, tb):
    t = idx_t.shape[0]
    return pl.pallas_call(
        functools.partial(_peer_v_kernel, tb=tb),
        out_shape=jax.ShapeDtypeStruct((t, D_MODEL), F32),
        grid=(t // tb,),
        in_specs=[pl.BlockSpec((tb, PEER_SLOTS), lambda i: (i, 0), memory_space=pltpu.SMEM),
                  pl.BlockSpec((tb, PEER_SLOTS), lambda i: (i, 0)),
                  _table_spec(tbl)],
        out_specs=pl.BlockSpec((tb, D_MODEL), lambda i: (i, 0)),
        scratch_shapes=[pltpu.VMEM((2, PEER_SLOTS * ROW_SUBLANES, 2 * LANES), jnp.uint32),
                        pltpu.VMEM((2, PEER_SLOTS * ROW_SUBLANES, 2 * LANES), jnp.uint32),
                        pltpu.VMEM((tb, 2 * ROW_SUBLANES * PEER_SLOTS), F32),
                        pltpu.VMEM((tb, 2 * ROW_SUBLANES * PEER_SLOTS), F32),
                        pltpu.VMEM((tb, SUBLANES, LANES), F32)],
        compiler_params=pltpu.CompilerParams(vmem_limit_bytes=VMEM_LIMIT),
        name="peer_v",
    )(idx_t, w_t, tbl)


def _final_kernel(h1_ref, p_ref, mod_ref, g_ref, o_ref):
    h = h1_ref[...] + mod_ref[0, 5:6, :] * p_ref[...]
    o_ref[...] = h * lax.rsqrt(jnp.mean(h * h, axis=-1, keepdims=True) + NORM_EPS) * g_ref[...]


def _final_call(h1, peer, mod3, g, seq, tm):
    t, d = h1.shape
    per_b = seq // tm
    row = lambda i: (i, 0)
    return pl.pallas_call(
        _final_kernel,
        out_shape=jax.ShapeDtypeStruct((t, d), F32),
        grid=(t // tm,),
        in_specs=[pl.BlockSpec((tm, d), row), pl.BlockSpec((tm, d), row),
                  pl.BlockSpec((1, 6, d), lambda i: (i // per_b, 0, 0)),
                  pl.BlockSpec((1, d), lambda i: (0, 0))],
        out_specs=pl.BlockSpec((tm, d), row),
        name="final",
    )(h1, peer, mod3, g.reshape(1, d))


def _pad_lanes(v):
    return jnp.pad(v.reshape(1, -1), ((0, 0), (0, LANES - v.shape[-1])))


def kernel(x, c, ada_w, ada_b, norm1_g, w_in, ssd_conv_w, ssd_conv_b, ssd_dt_bias, ssd_a_log, ssd_d, ssd_norm_g, conf_dw_w, conf_dw_b, conf_ln_g, conf_ln_b, w_out, norm2_g, peer_w_query, peer_sub_keys, peer_u, peer_v, final_norm_g):
    bsz, seq, d = x.shape
    assert d == D_MODEL and ada_w.shape[0] == 1
    t = bsz * seq
    tm = min(512, seq)
    ts = min(256, seq)
    tr = min(256, seq)
    tb = min(512, seq)
    x2 = x.reshape(t, d)

    mod3 = _mod_call(c, ada_w[0], ada_b[0]).reshape(bsz, 6, d)

    wi = w_in[0]
    o1 = SSD_WIDTH
    o2 = o1 + SSD_XBC
    o3 = o2 + SSD_HEADS
    wz = wi[:, :o1].astype(BF16)
    wx = wi[:, o1:o2].astype(BF16)
    wd = jnp.pad(wi[:, o2:o3], ((0, 0), (0, LANES - SSD_HEADS))).astype(BF16)
    wg = wi[:, o3:].astype(BF16)
    z, xbc, glu, dt = _inproj_call(x2, mod3, norm1_g[0], wz, wx, wg, wd, seq, tm)

    h1, hn2 = _mixer_call(
        x2, z, xbc, glu, dt, mod3,
        ssd_conv_w[0], ssd_conv_b[0].reshape(1, -1), _pad_lanes(ssd_dt_bias[0]), _pad_lanes(ssd_a_log[0]),
        jnp.repeat(ssd_d[0], SSD_HEAD_DIM).reshape(1, -1), ssd_norm_g[0].reshape(1, -1),
        conf_dw_w[0], conf_dw_b[0].reshape(1, -1), conf_ln_g[0].reshape(1, -1), conf_ln_b[0].reshape(1, -1),
        w_out[0].astype(BF16), norm2_g[0].reshape(1, -1), bsz, seq, ts)

    keys = peer_sub_keys[0].reshape(PEER_HEADS * 2, PEER_N_KEYS, PEER_D_HALF).astype(BF16)
    idx_t, gate_t = _route_call(hn2, peer_w_query[0].astype(BF16), keys, tr)

    w_t = _peer_u_call(idx_t, hn2, gate_t, _pack_table(peer_u[0]), tb)
    peer = _peer_v_call(idx_t, w_t, _pack_table(peer_v[0]), tb)

    out = _final_call(h1, peer, mod3, final_norm_g, seq, tm)
    return out.reshape(bsz, seq, d)
```

```python
import functools
import math

import jax
import jax.numpy as jnp
from jax import lax
from jax.experimental import pallas as pl
from jax.experimental.pallas import tpu as pltpu

F32 = jnp.float32
BF16 = jnp.bfloat16
HIGHEST = lax.Precision.HIGHEST

D_MODEL = 1024
CHUNK = 64
SSD_WIDTH = 512
SSD_HEADS = 8
SSD_HEAD_DIM = 64
SSD_GROUPS = 2
SSD_STATE = 128
SSD_CONV = 4
SSD_XBC = 1024
CONF_WIDTH = 512
CONF_CONV = 31
PEER_HEADS = 8
PEER_N_KEYS = 128
PEER_D_HALF = 128
PEER_TOPK = 16
PEER_SLOTS = PEER_HEADS * PEER_TOPK
NORM_EPS = 1e-6

LANES = 128
SUBLANES = 8
ROW_SUBLANES = D_MODEL // 2 // LANES
VMEM_LIMIT = 56 * 1024 * 1024

XBC_TAIL = 8
GLU_TAIL = 32


def _silu(v):
    return v * jax.nn.sigmoid(v)


def _softplus(v):
    return jnp.maximum(v, 0.0) + jnp.log(1.0 + jnp.exp(-jnp.abs(v)))


def _bdot(a, b):
    return jnp.dot(a.astype(BF16), b.astype(BF16), preferred_element_type=F32)


def _mod_kernel(c_ref, w_ref, b_ref, o_ref):
    cond = _silu(c_ref[...])
    o_ref[...] = jnp.dot(cond, w_ref[...], precision=HIGHEST, preferred_element_type=F32) + b_ref[...]


def _mod_call(c, ada_w, ada_b):
    bsz, d = c.shape
    n = ada_w.shape[1]
    return pl.pallas_call(
        _mod_kernel,
        out_shape=jax.ShapeDtypeStruct((bsz, n), F32),
        grid=(n // d,),
        in_specs=[pl.BlockSpec((bsz, d), lambda i: (0, 0)),
                  pl.BlockSpec((d, d), lambda i: (0, i)),
                  pl.BlockSpec((1, d), lambda i: (0, i))],
        out_specs=pl.BlockSpec((bsz, d), lambda i: (0, i)),
        name="mod",
    )(c, ada_w, ada_b.reshape(1, n))


def _inproj_kernel(x_ref, mod_ref, g_ref, wz_ref, wx_ref, wg_ref, wd_ref,
                   z_ref, xbc_ref, glu_ref, dt_ref):
    x = x_ref[...]
    ms = jnp.mean(x * x, axis=-1, keepdims=True)
    y = x * lax.rsqrt(ms + NORM_EPS) * g_ref[...]
    sh = mod_ref[0, 0:1, :]
    sc = mod_ref[0, 1:2, :]
    hn = (y * (1.0 + sc) + sh).astype(BF16)
    z_ref[...] = jnp.dot(hn, wz_ref[...], preferred_element_type=F32)
    xbc_ref[...] = jnp.dot(hn, wx_ref[...], preferred_element_type=F32)
    glu_ref[...] = jnp.dot(hn, wg_ref[...], preferred_element_type=F32)
    dt_ref[...] = jnp.dot(hn, wd_ref[...], preferred_element_type=F32)


def _inproj_call(x2, mod3, norm1_g, wz, wx, wg, wd, seq, tm):
    t, d = x2.shape
    per_b = seq // tm
    const = lambda i: (0, 0)
    row = lambda i: (i, 0)
    return pl.pallas_call(
        _inproj_kernel,
        out_shape=(jax.ShapeDtypeStruct((t, SSD_WIDTH), F32),
                   jax.ShapeDtypeStruct((t, SSD_XBC), F32),
                   jax.ShapeDtypeStruct((t, 2 * CONF_WIDTH), F32),
                   jax.ShapeDtypeStruct((t, LANES), F32)),
        grid=(t // tm,),
        in_specs=[pl.BlockSpec((tm, d), row),
                  pl.BlockSpec((1, 6, d), lambda i: (i // per_b, 0, 0)),
                  pl.BlockSpec((1, d), const),
                  pl.BlockSpec(wz.shape, const),
                  pl.BlockSpec(wx.shape, const),
                  pl.BlockSpec(wg.shape, const),
                  pl.BlockSpec(wd.shape, const)],
        out_specs=(pl.BlockSpec((tm, SSD_WIDTH), row),
                   pl.BlockSpec((tm, SSD_XBC), row),
                   pl.BlockSpec((tm, 2 * CONF_WIDTH), row),
                   pl.BlockSpec((tm, LANES), row)),
        compiler_params=pltpu.CompilerParams(vmem_limit_bytes=VMEM_LIMIT),
        name="inproj",
    )(x2, mod3, norm1_g.reshape(1, d), wz, wx, wg, wd)


def _mixer_kernel(x_ref, z_ref, xbc_ref, glu_ref, dt_ref, mod_ref,
                  cw_ref, cb_ref, dtb_ref, alog_ref, dexp_ref, sng_ref,
                  dww_ref, dwb_ref, lng_ref, lnb_ref, wout_ref, n2g_ref,
                  h1_ref, hn2_ref,
                  xext_ref, gext_ref, hst_ref, xc_ref, xdt_ref, acs_ref, eacs_ref,
                  acst_ref, bmt_ref, y_ref, gsh_ref, *, ts):
    nc = ts // CHUNK
    hw = SSD_WIDTH // SSD_GROUPS

    @pl.when(pl.program_id(1) == 0)
    def _():
        xext_ref[0:XBC_TAIL, :] = jnp.zeros((XBC_TAIL, SSD_XBC), F32)
        gext_ref[0:GLU_TAIL, :] = jnp.zeros((GLU_TAIL, CONF_WIDTH), F32)
        hst_ref[...] = jnp.zeros(hst_ref.shape, F32)

    xext_ref[XBC_TAIL:XBC_TAIL + ts, :] = xbc_ref[...]
    acc = cb_ref[...] + cw_ref[0:1, :] * xext_ref[pl.ds(XBC_TAIL - SSD_CONV + 1, ts), :]
    for k in range(1, SSD_CONV):
        acc = acc + cw_ref[k:k + 1, :] * xext_ref[pl.ds(XBC_TAIL - SSD_CONV + 1 + k, ts), :]
    xext_ref[0:XBC_TAIL, :] = xext_ref[ts:ts + XBC_TAIL, :]
    xc_ref[...] = _silu(acc)

    dt = _softplus(dt_ref[...] + dtb_ref[...])
    dta = dt * (-jnp.exp(alog_ref[...]))
    ri = lax.broadcasted_iota(jnp.int32, (ts, ts), 0)
    ci = lax.broadcasted_iota(jnp.int32, (ts, ts), 1)
    ltri = jnp.where(((ri // CHUNK) == (ci // CHUNK)) & (ci <= ri), 1.0, 0.0).astype(F32)
    acs = jnp.dot(ltri, dta, precision=HIGHEST, preferred_element_type=F32)
    acst_ref[...] = acs.T
    er = lax.broadcasted_iota(jnp.int32, (LANES, SSD_WIDTH), 0)
    ec = lax.broadcasted_iota(jnp.int32, (LANES, SSD_WIDTH), 1)
    expand = jnp.where((ec // SSD_HEAD_DIM) == er, 1.0, 0.0).astype(F32)
    dt_exp = jnp.dot(dt, expand, precision=HIGHEST, preferred_element_type=F32)
    acs_exp = jnp.dot(acs, expand, precision=HIGHEST, preferred_element_type=F32)
    acs_ref[...] = acs_exp
    eacs_ref[...] = jnp.exp(acs_exp)
    xdt_ref[...] = xc_ref[:, 0:SSD_WIDTH] * dt_exp
    bmt_ref[...] = xc_ref[:, SSD_WIDTH:SSD_WIDTH + SSD_GROUPS * SSD_STATE].T

    tr = lax.broadcasted_iota(jnp.int32, (CHUNK, CHUNK), 0)
    tc = lax.broadcasted_iota(jnp.int32, (CHUNK, CHUNK), 1)
    tril = tc <= tr
    c_off = SSD_WIDTH + SSD_GROUPS * SSD_STATE

    for c in range(nc):
        r0 = c * CHUNK
        rows = slice(r0, r0 + CHUNK)
        a_last = acs_ref[r0 + CHUNK - 1:r0 + CHUNK, :]
        xw = xdt_ref[rows, :] * jnp.exp(a_last - acs_ref[rows, :])
        cdec = jnp.exp(a_last)
        y_parts = []
        for g in range(SSD_GROUPS):
            cg = xc_ref[rows, c_off + g * SSD_STATE:c_off + (g + 1) * SSD_STATE].astype(BF16)
            bg = xc_ref[rows, SSD_WIDTH + g * SSD_STATE:SSD_WIDTH + (g + 1) * SSD_STATE].astype(BF16)
            cb = lax.dot_general(cg, bg, (((1,), (1,)), ((), ())), preferred_element_type=F32)
            hg = hst_ref[g]
            yoff = jnp.dot(cg, hg.astype(BF16), preferred_element_type=F32)
            st = _bdot(bmt_ref[g * SSD_STATE:(g + 1) * SSD_STATE, rows], xw[:, g * hw:(g + 1) * hw])
            hst_ref[g] = hg * cdec[:, g * hw:(g + 1) * hw] + st
            yds = []
            for hh in range(SSD_HEADS // SSD_GROUPS):
                h = g * (SSD_HEADS // SSD_GROUPS) + hh
                cols = slice(h * SSD_HEAD_DIM, (h + 1) * SSD_HEAD_DIM)
                seg = acs_ref[rows, cols] - acst_ref[h:h + 1, rows]
                dec = jnp.exp(jnp.where(tril, seg, -jnp.inf))
                yds.append(_bdot(cb * dec, xdt_ref[rows, cols]))
            y_parts.append(jnp.concatenate(yds, axis=1) + yoff * eacs_ref[rows, g * hw:(g + 1) * hw])
        y_ref[rows, :] = jnp.concatenate(y_parts, axis=1) + dexp_ref[...] * xc_ref[rows, 0:SSD_WIDTH]

    y = y_ref[...] * _silu(z_ref[...])
    y_ssd = y * lax.rsqrt(jnp.mean(y * y, axis=-1, keepdims=True) + NORM_EPS) * sng_ref[...]

    gext_ref[GLU_TAIL:GLU_TAIL + ts, :] = glu_ref[:, 0:CONF_WIDTH] * jax.nn.sigmoid(glu_ref[:, CONF_WIDTH:])
    span = ts + GLU_TAIL - SUBLANES
    for s in range(1, SUBLANES):
        gsh_ref[s - 1, 0:span, :] = gext_ref[pl.ds(s, span), :]
    u = dwb_ref[...]
    for k in range(CONF_CONV):
        off = GLU_TAIL - CONF_CONV + 1 + k
        s = off % SUBLANES
        rows = pl.ds(off - s, ts)
        u = u + dww_ref[k:k + 1, :] * (gext_ref[rows, :] if s == 0 else gsh_ref[s - 1, rows, :])
    gext_ref[0:GLU_TAIL, :] = gext_ref[ts:ts + GLU_TAIL, :]
    mu = jnp.mean(u, axis=-1, keepdims=True)
    uc = u - mu
    var = jnp.mean(uc * uc, axis=-1, keepdims=True)
    y_conf = _silu(uc * lax.rsqrt(var + NORM_EPS) * lng_ref[...] + lnb_ref[...])

    mix = (jnp.dot(y_ssd.astype(BF16), wout_ref[0:SSD_WIDTH, :], preferred_element_type=F32)
           + jnp.dot(y_conf.astype(BF16), wout_ref[SSD_WIDTH:, :], preferred_element_type=F32))
    h1 = x_ref[...] + mod_ref[0, 2:3, :] * mix
    h1_ref[...] = h1
    hn = h1 * lax.rsqrt(jnp.mean(h1 * h1, axis=-1, keepdims=True) + NORM_EPS) * n2g_ref[...]
    hn2_ref[...] = hn * (1.0 + mod_ref[0, 4:5, :]) + mod_ref[0, 3:4, :]


def _mixer_call(x2, z, xbc, glu, dt, mod3, cw, cb, dtb, alog, dexp, sng, dww, dwb, lng, lnb,
                wout, n2g, bsz, seq, ts):
    t, d = x2.shape
    per_b = seq // ts
    row = lambda b, j: (b * per_b + j, 0)
    const = lambda b, j: (0, 0)

    def full(a):
        return pl.BlockSpec(a.shape, const)

    return pl.pallas_call(
        functools.partial(_mixer_kernel, ts=ts),
        out_shape=(jax.ShapeDtypeStruct((t, d), F32), jax.ShapeDtypeStruct((t, d), F32)),
        grid=(bsz, per_b),
        in_specs=[pl.BlockSpec((ts, d), row),
                  pl.BlockSpec((ts, SSD_WIDTH), row),
                  pl.BlockSpec((ts, SSD_XBC), row),
                  pl.BlockSpec((ts, 2 * CONF_WIDTH), row),
                  pl.BlockSpec((ts, LANES), row),
                  pl.BlockSpec((1, 6, d), lambda b, j: (b, 0, 0)),
                  full(cw), full(cb), full(dtb), full(alog), full(dexp), full(sng),
                  full(dww), full(dwb), full(lng), full(lnb), full(wout), full(n2g)],
        out_specs=(pl.BlockSpec((ts, d), row), pl.BlockSpec((ts, d), row)),
        scratch_shapes=[pltpu.VMEM((ts + XBC_TAIL, SSD_XBC), F32),
                        pltpu.VMEM((ts + GLU_TAIL, CONF_WIDTH), F32),
                        pltpu.VMEM((SSD_GROUPS, SSD_STATE, SSD_WIDTH // SSD_GROUPS), F32),
                        pltpu.VMEM((ts, SSD_XBC), F32),
                        pltpu.VMEM((ts, SSD_WIDTH), F32),
                        pltpu.VMEM((ts, SSD_WIDTH), F32),
                        pltpu.VMEM((ts, SSD_WIDTH), F32),
                        pltpu.VMEM((LANES, ts), F32),
                        pltpu.VMEM((SSD_GROUPS * SSD_STATE, ts), F32),
                        pltpu.VMEM((ts, SSD_WIDTH), F32),
                        pltpu.VMEM((SUBLANES - 1, ts + GLU_TAIL - SUBLANES, CONF_WIDTH), F32)],
        compiler_params=pltpu.CompilerParams(
            dimension_semantics=("arbitrary", "arbitrary"), vmem_limit_bytes=VMEM_LIMIT),
        name="mixer",
    )(x2, z, xbc, glu, dt, mod3, cw, cb, dtb, alog, dexp, sng, dww, dwb, lng, lnb, wout, n2g)


_PAIR_COUNTS = tuple(PEER_TOPK // (a + 1) for a in range(PEER_TOPK))


def _topk_rows(s, val_ref, pick_ref, payload=None):
    n = s.shape[0]
    h = n // 2
    iota = lax.broadcasted_iota(jnp.int32, (h, s.shape[1]), 0).astype(F32)
    a, b = s[:h], s[h:]
    first = a >= b
    hi, lo = jnp.where(first, a, b), jnp.where(first, b, a)
    ihi, ilo = jnp.where(first, iota, iota + float(h)), jnp.where(first, iota + float(h), iota)
    if payload is not None:
        phi, plo = jnp.where(first, payload[:h], payload[h:]), jnp.where(first, payload[h:], payload[:h])
    for r in range(val_ref.shape[0]):
        m = jnp.max(hi, axis=0, keepdims=True)
        am = jnp.min(jnp.where(hi == m, ihi, float(n)), axis=0, keepdims=True)
        val_ref[r:r + 1, :] = m
        hit = ihi == am
        if payload is None:
            pick_ref[r:r + 1, :] = am
        else:
            pick_ref[r:r + 1, :] = jnp.max(jnp.where(hit, phi, -1.0), axis=0, keepdims=True)
            phi = jnp.where(hit, plo, phi)
        hi = jnp.where(hit, lo, hi)
        ihi = jnp.where(hit, ilo, ihi)
        lo = jnp.where(hit, -jnp.inf, lo)


def _route_kernel(hn_ref, wq_ref, keys_ref, idx_ref, gate_ref, q_ref, topv_ref, topi_ref, best_ref, exp_ref):
    q_ref[...] = jnp.dot(hn_ref[...].astype(BF16), wq_ref[...], preferred_element_type=F32)
    nt = (((1,), (1,)), ((), ()))
    n_cand = sum(_PAIR_COUNTS)
    n_pad = -n_cand % (2 * SUBLANES)
    for lt in range(q_ref.shape[0] // LANES):
        toks = slice(lt * LANES, (lt + 1) * LANES)
        for h in range(PEER_HEADS):
            for i in range(2):
                col = (h * 2 + i) * PEER_D_HALF
                qh = q_ref[toks, col:col + PEER_D_HALF].astype(BF16)
                sc = lax.dot_general(keys_ref[h * 2 + i], qh, nt, preferred_element_type=F32)
                _topk_rows(sc, topv_ref.at[i], topi_ref.at[i])
            sv1, sv2 = topv_ref[0], topv_ref[1]
            si1, si2 = topi_ref[0], topi_ref[1]
            cand = jnp.concatenate([sv1[a:a + 1] + sv2[0:nb] for a, nb in enumerate(_PAIR_COUNTS)]
                                   + [jnp.full((n_pad, LANES), -jnp.inf, F32)], axis=0)
            cidx = jnp.concatenate([si1[a:a + 1] * float(PEER_N_KEYS) + si2[0:nb] for a, nb in enumerate(_PAIR_COUNTS)]
                                   + [jnp.zeros((n_pad, LANES), F32)], axis=0)
            _topk_rows(cand, best_ref, exp_ref.at[pl.ds(h * PEER_TOPK, PEER_TOPK)], payload=cidx)
            best = best_ref[...]
            e = jnp.exp(best - best[0:1, :])
            gate_ref[h * PEER_TOPK:(h + 1) * PEER_TOPK, toks] = e / jnp.sum(e, axis=0, keepdims=True)
        idx_ref[toks, :] = (exp_ref[...].T * float(ROW_SUBLANES)).astype(jnp.int32)


def _route_call(hn2, wq, keys, tm):
    t, d = hn2.shape
    return pl.pallas_call(
        _route_kernel,
        out_shape=(jax.ShapeDtypeStruct((t, PEER_SLOTS), jnp.int32),
                   jax.ShapeDtypeStruct((PEER_SLOTS, t), F32)),
        grid=(t // tm,),
        in_specs=[pl.BlockSpec((tm, d), lambda i: (i, 0)),
                  pl.BlockSpec(wq.shape, lambda i: (0, 0)),
                  pl.BlockSpec(keys.shape, lambda i: (0, 0, 0))],
        out_specs=(pl.BlockSpec((tm, PEER_SLOTS), lambda i: (i, 0)),
                   pl.BlockSpec((PEER_SLOTS, tm), lambda i: (0, i))),
        scratch_shapes=[pltpu.VMEM((tm, wq.shape[1]), F32),
                        pltpu.VMEM((2, PEER_TOPK, LANES), F32),
                        pltpu.VMEM((2, PEER_TOPK, LANES), F32),
                        pltpu.VMEM((PEER_TOPK, LANES), F32),
                        pltpu.VMEM((PEER_SLOTS, LANES), F32)],
        compiler_params=pltpu.CompilerParams(vmem_limit_bytes=VMEM_LIMIT),
        name="route",
    )(hn2, wq, keys)


def _pack_kernel(t_ref, o_ref):
    half = t_ref.shape[1] // 2
    lo = pltpu.bitcast(t_ref[:, :half].astype(BF16).astype(F32), jnp.uint32)
    hi = pltpu.bitcast(t_ref[:, half:].astype(BF16).astype(F32), jnp.uint32)
    word = hi | (lo >> 16)
    rows = t_ref.shape[0]
    for s in range(ROW_SUBLANES):
        o_ref[pl.ds(s, rows, stride=ROW_SUBLANES), :] = word[:, s * LANES:(s + 1) * LANES]


def _pack_table(tbl):
    n, d = tbl.shape
    rows = 512
    return pl.pallas_call(
        _pack_kernel,
        out_shape=jax.ShapeDtypeStruct((n * ROW_SUBLANES, LANES), jnp.uint32),
        grid=(n // rows,),
        in_specs=[pl.BlockSpec((rows, d), lambda i: (i, 0))],
        out_specs=pl.BlockSpec((rows * ROW_SUBLANES, LANES), lambda i: (i, 0)),
        name="pack",
    )(tbl)


def _unpack_lo(w):
    return pltpu.bitcast(w << 16, F32)


def _unpack_hi(w):
    return pltpu.bitcast(w & jnp.uint32(0xFFFF0000), F32)


def _gather_rows(idx_ref, tbl_ref, slot_ref, t):
    for j in range(PEER_SLOTS):
        start = pl.multiple_of(idx_ref[t, j], ROW_SUBLANES)
        slot_ref[j * ROW_SUBLANES:(j + 1) * ROW_SUBLANES, :] = tbl_ref[pl.ds(start, ROW_SUBLANES), :]


def _peer_u_kernel(idx_ref, x_ref, gate_ref, tbl_ref, w_ref, slot_a, slot_b, prod_a, prod_b, x3_ref, *, tb):
    per_vreg = SUBLANES // ROW_SUBLANES
    lane = lax.broadcasted_iota(jnp.int32, (PEER_SLOTS, LANES), 1)
    for r in range(SUBLANES):
        x3_ref[:, r, :] = x_ref[:, r * LANES:(r + 1) * LANES]

    def products(slot_ref, prod_ref, t):
        xt = x3_ref[t]
        xlo = jnp.concatenate([xt[0:ROW_SUBLANES]] * per_vreg, axis=0)
        xhi = jnp.concatenate([xt[ROW_SUBLANES:]] * per_vreg, axis=0)
        words = slot_ref[...].reshape(PEER_SLOTS // per_vreg, SUBLANES, LANES)
        prod = _unpack_lo(words) * xlo[None] + _unpack_hi(words) * xhi[None]
        prod_ref[...] = prod.reshape(PEER_SLOTS * ROW_SUBLANES, LANES)

    def reduce_into(prod_ref, tl, acc):
        part = prod_ref[pl.ds(0, PEER_SLOTS, stride=ROW_SUBLANES), :]
        for r in range(1, ROW_SUBLANES):
            part = part + prod_ref[pl.ds(r, PEER_SLOTS, stride=ROW_SUBLANES), :]
        col = jnp.sum(part, axis=-1, keepdims=True)
        return jnp.where(lane == tl, col, acc)

    prod_a[...] = jnp.zeros(prod_a.shape, F32)
    prod_b[...] = jnp.zeros(prod_b.shape, F32)
    for blk in range(tb // LANES):
        base = blk * LANES
        _gather_rows(idx_ref, tbl_ref, slot_a, base)
        _gather_rows(idx_ref, tbl_ref, slot_b, base + 1)

        def body(i, acc):
            tl = 2 * i
            acc = reduce_into(prod_a, tl - 2, acc)
            acc = reduce_into(prod_b, tl - 1, acc)
            products(slot_a, prod_a, base + tl)
            _gather_rows(idx_ref, tbl_ref, slot_a, jnp.minimum(base + tl + 2, tb - 1))
            products(slot_b, prod_b, base + tl + 1)
            _gather_rows(idx_ref, tbl_ref, slot_b, jnp.minimum(base + tl + 3, tb - 1))
            return acc

        act = lax.fori_loop(0, LANES // 2, body, jnp.zeros((PEER_SLOTS, LANES), F32))
        act = reduce_into(prod_a, LANES - 2, act)
        act = reduce_into(prod_b, LANES - 1, act)
        gelu = 0.5 * act * (1.0 + lax.erf(act * (1.0 / math.sqrt(2.0))))
        w_ref[blk * LANES:(blk + 1) * LANES, :] = (gate_ref[:, blk * LANES:(blk + 1) * LANES] * gelu).T


def _peer_v_kernel(idx_ref, w_ref, tbl_ref, o_ref, slot_a, slot_b, wrep_hi, wrep_lo, o3_ref, *, tb):
    cols = 2 * ROW_SUBLANES * PEER_SLOTS
    w = w_ref[...]
    hi = w.astype(BF16)
    lo = (w - hi.astype(F32)).astype(BF16)
    jr = lax.broadcasted_iota(jnp.int32, (PEER_SLOTS, cols), 0)
    jc = lax.broadcasted_iota(jnp.int32, (PEER_SLOTS, cols), 1)
    expand = jnp.where(jc // (2 * ROW_SUBLANES) == jr, 1.0, 0.0).astype(BF16)
    wrep_hi[...] = jnp.dot(hi, expand, preferred_element_type=F32)
    wrep_lo[...] = jnp.dot(lo, expand, preferred_element_type=F32)
    rr = lax.broadcasted_iota(jnp.int32, (SUBLANES, cols), 0)
    rc = lax.broadcasted_iota(jnp.int32, (SUBLANES, cols), 1)
    mask = (rc % (2 * ROW_SUBLANES)) == 2 * (rr % ROW_SUBLANES) + rr // ROW_SUBLANES

    def lhs_rows(t):
        return [jnp.where(mask, jnp.broadcast_to(ref[pl.ds(t, 1), :], (SUBLANES, cols)), 0.0)
                for ref in (wrep_hi, wrep_lo)]

    def gather_pair(slot_ref, t):
        for k in range(2):
            for j in range(PEER_SLOTS):
                start = pl.multiple_of(idx_ref[t + k, j], ROW_SUBLANES)
                slot_ref[j * ROW_SUBLANES:(j + 1) * ROW_SUBLANES, k * LANES:(k + 1) * LANES] = (
                    tbl_ref[pl.ds(start, ROW_SUBLANES), :])

    def store_token(t, val):
        o3_ref[t] = val

    def combine_pair(slot_ref, t):
        lhs = jnp.concatenate(lhs_rows(t) + lhs_rows(t + 1), axis=0).astype(BF16)
        res = jnp.dot(lhs, pltpu.bitcast(slot_ref[...], BF16), preferred_element_type=F32)
        store_token(t, res[0:SUBLANES, 0:LANES] + res[SUBLANES:2 * SUBLANES, 0:LANES])
        store_token(t + 1, res[2 * SUBLANES:3 * SUBLANES, LANES:] + res[3 * SUBLANES:, LANES:])

    pairs = slot_a.shape[0]
    half = 2 * pairs
    for q in range(pairs):
        gather_pair(slot_a.at[q], 2 * q)
        gather_pair(slot_b.at[q], half + 2 * q)

    def body(i, carry):
        t0 = 2 * half * i
        for q in range(pairs):
            combine_pair(slot_a.at[q], t0 + 2 * q)
        for q in range(pairs):
            gather_pair(slot_a.at[q], jnp.minimum(t0 + 2 * half + 2 * q, tb - 2))
        for q in range(pairs):
            combine_pair(slot_b.at[q], t0 + half + 2 * q)
        for q in range(pairs):
            gather_pair(slot_b.at[q], jnp.minimum(t0 + 3 * half + 2 * q, tb - 2))
        return carry

    lax.fori_loop(0, tb // (2 * half), body, 0)
    for r in range(SUBLANES):
        o_ref[:, r * LANES:(r + 1) * LANES] = o3_ref[:, r, :]


def _table_spec(tbl):
    return pl.BlockSpec(tbl.shape, lambda i: (0, 0), pipeline_mode=pl.Buffered(1))


def _peer_u_call(idx_t, x2, gate_t, tbl, tb):
    t, d = x2.shape
    return pl.pallas_call(
        functools.partial(_peer_u_kernel, tb=tb),
        out_shape=jax.ShapeDtypeStruct((t, PEER_SLOTS), F32),
        grid=(t // tb,),
        in_specs=[pl.BlockSpec((tb, PEER_SLOTS), lambda i: (i, 0), memory_space=pltpu.SMEM),
                  pl.BlockSpec((tb, d), lambda i: (i, 0)),
                  pl.BlockSpec((PEER_SLOTS, tb), lambda i: (0, i)),
                  _table_spec(tbl)],
        out_specs=pl.BlockSpec((tb, PEER_SLOTS), lambda i: (i, 0)),
        scratch_shapes=[pltpu.VMEM((PEER_SLOTS * ROW_SUBLANES, LANES), jnp.uint32),
                        pltpu.VMEM((PEER_SLOTS * ROW_SUBLANES, LANES), jnp.uint32),
                        pltpu.VMEM((PEER_SLOTS * ROW_SUBLANES, LANES), F32),
                        pltpu.VMEM((PEER_SLOTS * ROW_SUBLANES, LANES), F32),
                        pltpu.VMEM((tb, SUBLANES, LANES), F32)],
        compiler_params=pltpu.CompilerParams(vmem_limit_bytes=VMEM_LIMIT),
        name="peer_u",
    )(idx_t, x2, gate_t, tbl)


def _peer_v_call(idx_t, w_t, tbl, tb):
    t = idx_t.shape[0]
    return pl.pallas_call(
        functools.partial(_peer_v_kernel, tb=tb),
        out_shape=jax.ShapeDtypeStruct((t, D_MODEL), F32),
        grid=(t // tb,),
        in_specs=[pl.BlockSpec((tb, PEER_SLOTS), lambda i: (i, 0), memory_space=pltpu.SMEM),
                  pl.BlockSpec((tb, PEER_SLOTS), lambda i: (i, 0)),
                  _table_spec(tbl)],
        out_specs=pl.BlockSpec((tb, D_MODEL), lambda i: (i, 0)),
        scratch_shapes=[pltpu.VMEM((2, PEER_SLOTS * ROW_SUBLANES, 2 * LANES), jnp.uint32),
                        pltpu.VMEM((2, PEER_SLOTS * ROW_SUBLANES, 2 * LANES), jnp.uint32),
                        pltpu.VMEM((tb, 2 * ROW_SUBLANES * PEER_SLOTS), F32),
                        pltpu.VMEM((tb, 2 * ROW_SUBLANES * PEER_SLOTS), F32),
                        pltpu.VMEM((tb, SUBLANES, LANES), F32)],
        compiler_params=pltpu.CompilerParams(vmem_limit_bytes=VMEM_LIMIT),
        name="peer_v",
    )(idx_t, w_t, tbl)


def _final_kernel(h1_ref, p_ref, mod_ref, g_ref, o_ref):
    h = h1_ref[...] + mod_ref[0, 5:6, :] * p_ref[...]
    o_ref[...] = h * lax.rsqrt(jnp.mean(h * h, axis=-1, keepdims=True) + NORM_EPS) * g_ref[...]


def _final_call(h1, peer, mod3, g, seq, tm):
    t, d = h1.shape
    per_b = seq // tm
    row = lambda i: (i, 0)
    return pl.pallas_call(
        _final_kernel,
        out_shape=jax.ShapeDtypeStruct((t, d), F32),
        grid=(t // tm,),
        in_specs=[pl.BlockSpec((tm, d), row), pl.BlockSpec((tm, d), row),
                  pl.BlockSpec((1, 6, d), lambda i: (i // per_b, 0, 0)),
                  pl.BlockSpec((1, d), lambda i: (0, 0))],
        out_specs=pl.BlockSpec((tm, d), row),
        name="final",
    )(h1, peer, mod3, g.reshape(1, d))


def _pad_lanes(v):
    return jnp.pad(v.reshape(1, -1), ((0, 0), (0, LANES - v.shape[-1])))


def kernel(x, c, ada_w, ada_b, norm1_g, w_in, ssd_conv_w, ssd_conv_b, ssd_dt_bias, ssd_a_log, ssd_d, ssd_norm_g, conf_dw_w, conf_dw_b, conf_ln_g, conf_ln_b, w_out, norm2_g, peer_w_query, peer_sub_keys, peer_u, peer_v, final_norm_g):
    bsz, seq, d = x.shape
    assert d == D_MODEL and ada_w.shape[0] == 1
    t = bsz * seq
    tm = min(512, seq)
    ts = min(256, seq)
    tr = min(256, seq)
    tb = min(512, seq)
    x2 = x.reshape(t, d)

    mod3 = _mod_call(c, ada_w[0], ada_b[0]).reshape(bsz, 6, d)

    wi = w_in[0]
    o1 = SSD_WIDTH
    o2 = o1 + SSD_XBC
    o3 = o2 + SSD_HEADS
    wz = wi[:, :o1].astype(BF16)
    wx = wi[:, o1:o2].astype(BF16)
    wd = jnp.pad(wi[:, o2:o3], ((0, 0), (0, LANES - SSD_HEADS))).astype(BF16)
    wg = wi[:, o3:].astype(BF16)
    z, xbc, glu, dt = _inproj_call(x2, mod3, norm1_g[0], wz, wx, wg, wd, seq, tm)

    h1, hn2 = _mixer_call(
        x2, z, xbc, glu, dt, mod3,
        ssd_conv_w[0], ssd_conv_b[0].reshape(1, -1), _pad_lanes(ssd_dt_bias[0]), _pad_lanes(ssd_a_log[0]),
        jnp.repeat(ssd_d[0], SSD_HEAD_DIM).reshape(1, -1), ssd_norm_g[0].reshape(1, -1),
        conf_dw_w[0], conf_dw_b[0].reshape(1, -1), conf_ln_g[0].reshape(1, -1), conf_ln_b[0].reshape(1, -1),
        w_out[0].astype(BF16), norm2_g[0].reshape(1, -1), bsz, seq, ts)

    keys = peer_sub_keys[0].reshape(PEER_HEADS * 2, PEER_N_KEYS, PEER_D_HALF).astype(BF16)
    idx_t, gate_t = _route_call(hn2, peer_w_query[0].astype(BF16), keys, tr)

    w_t = _peer_u_call(idx_t, hn2, gate_t, _pack_table(peer_u[0]), tb)
    peer = _peer_v_call(idx_t, w_t, _pack_table(peer_v[0]), tb)

    out = _final_call(h1, peer, mod3, final_norm_g, seq, tm)
    return out.reshape(bsz, seq, d)
```

```python
import functools
import math

import jax
import jax.numpy as jnp
from jax import lax
from jax.experimental import pallas as pl
from jax.experimental.pallas import tpu as pltpu

F32 = jnp.float32
BF16 = jnp.bfloat16
HIGHEST = lax.Precision.HIGHEST

D_MODEL = 1024
CHUNK = 64
SSD_WIDTH = 512
SSD_HEADS = 8
SSD_HEAD_DIM = 64
SSD_GROUPS = 2
SSD_STATE = 128
SSD_CONV = 4
SSD_XBC = 1024
CONF_WIDTH = 512
CONF_CONV = 31
PEER_HEADS = 8
PEER_N_KEYS = 128
PEER_D_HALF = 128
PEER_TOPK = 16
PEER_SLOTS = PEER_HEADS * PEER_TOPK
NORM_EPS = 1e-6

LANES = 128
SUBLANES = 8
ROW_SUBLANES = D_MODEL // 2 // LANES
VMEM_LIMIT = 56 * 1024 * 1024

XBC_TAIL = 8
GLU_TAIL = 32


def _silu(v):
    return v * jax.nn.sigmoid(v)


def _softplus(v):
    return jnp.maximum(v, 0.0) + jnp.log(1.0 + jnp.exp(-jnp.abs(v)))


def _bdot(a, b):
    return jnp.dot(a.astype(BF16), b.astype(BF16), preferred_element_type=F32)


def _mod_kernel(c_ref, w_ref, b_ref, o_ref):
    cond = _silu(c_ref[...])
    o_ref[...] = jnp.dot(cond, w_ref[...], precision=HIGHEST, preferred_element_type=F32) + b_ref[...]


def _mod_call(c, ada_w, ada_b):
    bsz, d = c.shape
    n = ada_w.shape[1]
    return pl.pallas_call(
        _mod_kernel,
        out_shape=jax.ShapeDtypeStruct((bsz, n), F32),
        grid=(n // d,),
        in_specs=[pl.BlockSpec((bsz, d), lambda i: (0, 0)),
                  pl.BlockSpec((d, d), lambda i: (0, i)),
                  pl.BlockSpec((1, d), lambda i: (0, i))],
        out_specs=pl.BlockSpec((bsz, d), lambda i: (0, i)),
        name="mod",
    )(c, ada_w, ada_b.reshape(1, n))


def _inproj_kernel(x_ref, mod_ref, g_ref, wz_ref, wx_ref, wg_ref, wd_ref,
                   z_ref, xbc_ref, glu_ref, dt_ref):
    x = x_ref[...]
    ms = jnp.mean(x * x, axis=-1, keepdims=True)
    y = x * lax.rsqrt(ms + NORM_EPS) * g_ref[...]
    sh = mod_ref[0, 0:1, :]
    sc = mod_ref[0, 1:2, :]
    hn = (y * (1.0 + sc) + sh).astype(BF16)
    z_ref[...] = jnp.dot(hn, wz_ref[...], preferred_element_type=F32)
    xbc_ref[...] = jnp.dot(hn, wx_ref[...], preferred_element_type=F32)
    glu_ref[...] = jnp.dot(hn, wg_ref[...], preferred_element_type=F32)
    dt_ref[...] = jnp.dot(hn, wd_ref[...], preferred_element_type=F32)


def _inproj_call(x2, mod3, norm1_g, wz, wx, wg, wd, seq, tm):
    t, d = x2.shape
    per_b = seq // tm
    const = lambda i: (0, 0)
    row = lambda i: (i, 0)
    return pl.pallas_call(
        _inproj_kernel,
        out_shape=(jax.ShapeDtypeStruct((t, SSD_WIDTH), F32),
                   jax.ShapeDtypeStruct((t, SSD_XBC), F32),
                   jax.ShapeDtypeStruct((t, 2 * CONF_WIDTH), F32),
                   jax.ShapeDtypeStruct((t, LANES), F32)),
        grid=(t // tm,),
        in_specs=[pl.BlockSpec((tm, d), row),
                  pl.BlockSpec((1, 6, d), lambda i: (i // per_b, 0, 0)),
                  pl.BlockSpec((1, d), const),
                  pl.BlockSpec(wz.shape, const),
                  pl.BlockSpec(wx.shape, const),
                  pl.BlockSpec(wg.shape, const),
                  pl.BlockSpec(wd.shape, const)],
        out_specs=(pl.BlockSpec((tm, SSD_WIDTH), row),
                   pl.BlockSpec((tm, SSD_XBC), row),
                   pl.BlockSpec((tm, 2 * CONF_WIDTH), row),
                   pl.BlockSpec((tm, LANES), row)),
        compiler_params=pltpu.CompilerParams(vmem_limit_bytes=VMEM_LIMIT),
        name="inproj",
    )(x2, mod3, norm1_g.reshape(1, d), wz, wx, wg, wd)


def _mixer_kernel(x_ref, z_ref, xbc_ref, glu_ref, dt_ref, mod_ref,
                  cw_ref, cb_ref, dtb_ref, alog_ref, dexp_ref, sng_ref,
                  dww_ref, dwb_ref, lng_ref, lnb_ref, wout_ref, n2g_ref,
                  h1_ref, hn2_ref,
                  xext_ref, gext_ref, hst_ref, xc_ref, xdt_ref, acs_ref, eacs_ref,
                  acst_ref, bmt_ref, y_ref, gsh_ref, *, ts):
    nc = ts // CHUNK
    hw = SSD_WIDTH // SSD_GROUPS

    @pl.when(pl.program_id(1) == 0)
    def _():
        xext_ref[0:XBC_TAIL, :] = jnp.zeros((XBC_TAIL, SSD_XBC), F32)
        gext_ref[0:GLU_TAIL, :] = jnp.zeros((GLU_TAIL, CONF_WIDTH), F32)
        hst_ref[...] = jnp.zeros(hst_ref.shape, F32)

    xext_ref[XBC_TAIL:XBC_TAIL + ts, :] = xbc_ref[...]
    acc = cb_ref[...] + cw_ref[0:1, :] * xext_ref[pl.ds(XBC_TAIL - SSD_CONV + 1, ts), :]
    for k in range(1, SSD_CONV):
        acc = acc + cw_ref[k:k + 1, :] * xext_ref[pl.ds(XBC_TAIL - SSD_CONV + 1 + k, ts), :]
    xext_ref[0:XBC_TAIL, :] = xext_ref[ts:ts + XBC_TAIL, :]
    xc_ref[...] = _silu(acc)

    dt = _softplus(dt_ref[...] + dtb_ref[...])
    dta = dt * (-jnp.exp(alog_ref[...]))
    ri = lax.broadcasted_iota(jnp.int32, (ts, ts), 0)
    ci = lax.broadcasted_iota(jnp.int32, (ts, ts), 1)
    ltri = jnp.where(((ri // CHUNK) == (ci // CHUNK)) & (ci <= ri), 1.0, 0.0).astype(F32)
    acs = jnp.dot(ltri, dta, precision=HIGHEST, preferred_element_type=F32)
    acst_ref[...] = acs.T
    er = lax.broadcasted_iota(jnp.int32, (LANES, SSD_WIDTH), 0)
    ec = lax.broadcasted_iota(jnp.int32, (LANES, SSD_WIDTH), 1)
    expand = jnp.where((ec // SSD_HEAD_DIM) == er, 1.0, 0.0).astype(F32)
    dt_exp = jnp.dot(dt, expand, precision=HIGHEST, preferred_element_type=F32)
    acs_exp = jnp.dot(acs, expand, precision=HIGHEST, preferred_element_type=F32)
    acs_ref[...] = acs_exp
    eacs_ref[...] = jnp.exp(acs_exp)
    xdt_ref[...] = xc_ref[:, 0:SSD_WIDTH] * dt_exp
    bmt_ref[...] = xc_ref[:, SSD_WIDTH:SSD_WIDTH + SSD_GROUPS * SSD_STATE].T

    tr = lax.broadcasted_iota(jnp.int32, (CHUNK, CHUNK), 0)
    tc = lax.broadcasted_iota(jnp.int32, (CHUNK, CHUNK), 1)
    tril = tc <= tr
    c_off = SSD_WIDTH + SSD_GROUPS * SSD_STATE

    for c in range(nc):
        r0 = c * CHUNK
        rows = slice(r0, r0 + CHUNK)
        a_last = acs_ref[r0 + CHUNK - 1:r0 + CHUNK, :]
        xw = xdt_ref[rows, :] * jnp.exp(a_last - acs_ref[rows, :])
        cdec = jnp.exp(a_last)
        y_parts = []
        for g in range(SSD_GROUPS):
            cg = xc_ref[rows, c_off + g * SSD_STATE:c_off + (g + 1) * SSD_STATE].astype(BF16)
            bg = xc_ref[rows, SSD_WIDTH + g * SSD_STATE:SSD_WIDTH + (g + 1) * SSD_STATE].astype(BF16)
            cb = lax.dot_general(cg, bg, (((1,), (1,)), ((), ())), preferred_element_type=F32)
            hg = hst_ref[g]
            yoff = jnp.dot(cg, hg.astype(BF16), preferred_element_type=F32)
            st = _bdot(bmt_ref[g * SSD_STATE:(g + 1) * SSD_STATE, rows], xw[:, g * hw:(g + 1) * hw])
            hst_ref[g] = hg * cdec[:, g * hw:(g + 1) * hw] + st
            yds = []
            for hh in range(SSD_HEADS // SSD_GROUPS):
                h = g * (SSD_HEADS // SSD_GROUPS) + hh
                cols = slice(h * SSD_HEAD_DIM, (h + 1) * SSD_HEAD_DIM)
                seg = acs_ref[rows, cols] - acst_ref[h:h + 1, rows]
                dec = jnp.exp(jnp.where(tril, seg, -jnp.inf))
                yds.append(_bdot(cb * dec, xdt_ref[rows, cols]))
            y_parts.append(jnp.concatenate(yds, axis=1) + yoff * eacs_ref[rows, g * hw:(g + 1) * hw])
        y_ref[rows, :] = jnp.concatenate(y_parts, axis=1) + dexp_ref[...] * xc_ref[rows, 0:SSD_WIDTH]

    y = y_ref[...] * _silu(z_ref[...])
    y_ssd = y * lax.rsqrt(jnp.mean(y * y, axis=-1, keepdims=True) + NORM_EPS) * sng_ref[...]

    gext_ref[GLU_TAIL:GLU_TAIL + ts, :] = glu_ref[:, 0:CONF_WIDTH] * jax.nn.sigmoid(glu_ref[:, CONF_WIDTH:])
    span = ts + GLU_TAIL - SUBLANES
    for s in range(1, SUBLANES):
        gsh_ref[s - 1, 0:span, :] = gext_ref[pl.ds(s, span), :]
    u = dwb_ref[...]
    for k in range(CONF_CONV):
        off = GLU_TAIL - CONF_CONV + 1 + k
        s = off % SUBLANES
        rows = pl.ds(off - s, ts)
        u = u + dww_ref[k:k + 1, :] * (gext_ref[rows, :] if s == 0 else gsh_ref[s - 1, rows, :])
    gext_ref[0:GLU_TAIL, :] = gext_ref[ts:ts + GLU_TAIL, :]
    mu = jnp.mean(u, axis=-1, keepdims=True)
    uc = u - mu
    var = jnp.mean(uc * uc, axis=-1, keepdims=True)
    y_conf = _silu(uc * lax.rsqrt(var + NORM_EPS) * lng_ref[...] + lnb_ref[...])

    mix = (jnp.dot(y_ssd.astype(BF16), wout_ref[0:SSD_WIDTH, :], preferred_element_type=F32)
           + jnp.dot(y_conf.astype(BF16), wout_ref[SSD_WIDTH:, :], preferred_element_type=F32))
    h1 = x_ref[...] + mod_ref[0, 2:3, :] * mix
    h1_ref[...] = h1
    hn = h1 * lax.rsqrt(jnp.mean(h1 * h1, axis=-1, keepdims=True) + NORM_EPS) * n2g_ref[...]
    hn2_ref[...] = hn * (1.0 + mod_ref[0, 4:5, :]) + mod_ref[0, 3:4, :]


def _mixer_call(x2, z, xbc, glu, dt, mod3, cw, cb, dtb, alog, dexp, sng, dww, dwb, lng, lnb,
                wout, n2g, bsz, seq, ts):
    t, d = x2.shape
    per_b = seq // ts
    row = lambda b, j: (b * per_b + j, 0)
    const = lambda b, j: (0, 0)

    def full(a):
        return pl.BlockSpec(a.shape, const)

    return pl.pallas_call(
        functools.partial(_mixer_kernel, ts=ts),
        out_shape=(jax.ShapeDtypeStruct((t, d), F32), jax.ShapeDtypeStruct((t, d), F32)),
        grid=(bsz, per_b),
        in_specs=[pl.BlockSpec((ts, d), row),
                  pl.BlockSpec((ts, SSD_WIDTH), row),
                  pl.BlockSpec((ts, SSD_XBC), row),
                  pl.BlockSpec((ts, 2 * CONF_WIDTH), row),
                  pl.BlockSpec((ts, LANES), row),
                  pl.BlockSpec((1, 6, d), lambda b, j: (b, 0, 0)),
                  full(cw), full(cb), full(dtb), full(alog), full(dexp), full(sng),
                  full(dww), full(dwb), full(lng), full(lnb), full(wout), full(n2g)],
        out_specs=(pl.BlockSpec((ts, d), row), pl.BlockSpec((ts, d), row)),
        scratch_shapes=[pltpu.VMEM((ts + XBC_TAIL, SSD_XBC), F32),
                        pltpu.VMEM((ts + GLU_TAIL, CONF_WIDTH), F32),
                        pltpu.VMEM((SSD_GROUPS, SSD_STATE, SSD_WIDTH // SSD_GROUPS), F32),
                        pltpu.VMEM((ts, SSD_XBC), F32),
                        pltpu.VMEM((ts, SSD_WIDTH), F32),
                        pltpu.VMEM((ts, SSD_WIDTH), F32),
                        pltpu.VMEM((ts, SSD_WIDTH), F32),
                        pltpu.VMEM((LANES, ts), F32),
                        pltpu.VMEM((SSD_GROUPS * SSD_STATE, ts), F32),
                        pltpu.VMEM((ts, SSD_WIDTH), F32),
                        pltpu.VMEM((SUBLANES - 1, ts + GLU_TAIL - SUBLANES, CONF_WIDTH), F32)],
        compiler_params=pltpu.CompilerParams(
            dimension_semantics=("arbitrary", "arbitrary"), vmem_limit_bytes=VMEM_LIMIT),
        name="mixer",
    )(x2, z, xbc, glu, dt, mod3, cw, cb, dtb, alog, dexp, sng, dww, dwb, lng, lnb, wout, n2g)


_PAIR_COUNTS = tuple(PEER_TOPK // (a + 1) for a in range(PEER_TOPK))


def _topk_rows(s, val_ref, pick_ref, payload=None):
    n = s.shape[0]
    h = n // 2
    iota = lax.broadcasted_iota(jnp.int32, (h, s.shape[1]), 0).astype(F32)
    a, b = s[:h], s[h:]
    first = a >= b
    hi, lo = jnp.where(first, a, b), jnp.where(first, b, a)
    ihi, ilo = jnp.where(first, iota, iota + float(h)), jnp.where(first, iota + float(h), iota)
    if payload is not None:
        phi, plo = jnp.where(first, payload[:h], payload[h:]), jnp.where(first, payload[h:], payload[:h])
    for r in range(val_ref.shape[0]):
        m = jnp.max(hi, axis=0, keepdims=True)
        am = jnp.min(jnp.where(hi == m, ihi, float(n)), axis=0, keepdims=True)
        val_ref[r:r + 1, :] = m
        hit = ihi == am
        if payload is None:
            pick_ref[r:r + 1, :] = am
        else:
            pick_ref[r:r + 1, :] = jnp.max(jnp.where(hit, phi, -1.0), axis=0, keepdims=True)
            phi = jnp.where(hit, plo, phi)
        hi = jnp.where(hit, lo, hi)
        ihi = jnp.where(hit, ilo, ihi)
        lo = jnp.where(hit, -jnp.inf, lo)


def _route_kernel(hn_ref, wq_ref, keys_ref, idx_ref, gate_ref, q_ref, topv_ref, topi_ref, best_ref, exp_ref):
    q_ref[...] = jnp.dot(hn_ref[...].astype(BF16), wq_ref[...], preferred_element_type=F32)
    nt = (((1,), (1,)), ((), ()))
    n_cand = sum(_PAIR_COUNTS)
    n_pad = -n_cand % (2 * SUBLANES)
    for lt in range(q_ref.shape[0] // LANES):
        toks = slice(lt * LANES, (lt + 1) * LANES)
        for h in range(PEER_HEADS):
            for i in range(2):
                col = (h * 2 + i) * PEER_D_HALF
                qh = q_ref[toks, col:col + PEER_D_HALF].astype(BF16)
                sc = lax.dot_general(keys_ref[h * 2 + i], qh, nt, preferred_element_type=F32)
                _topk_rows(sc, topv_ref.at[i], topi_ref.at[i])
            sv1, sv2 = topv_ref[0], topv_ref[1]
            si1, si2 = topi_ref[0], topi_ref[1]
            cand = jnp.concatenate([sv1[a:a + 1] + sv2[0:nb] for a, nb in enumerate(_PAIR_COUNTS)]
                                   + [jnp.full((n_pad, LANES), -jnp.inf, F32)], axis=0)
            cidx = jnp.concatenate([si1[a:a + 1] * float(PEER_N_KEYS) + si2[0:nb] for a, nb in enumerate(_PAIR_COUNTS)]
                                   + [jnp.zeros((n_pad, LANES), F32)], axis=0)
            _topk_rows(cand, best_ref, exp_ref.at[pl.ds(h * PEER_TOPK, PEER_TOPK)], payload=cidx)
            best = best_ref[...]
            e = jnp.exp(best - best[0:1, :])
            gate_ref[h * PEER_TOPK:(h + 1) * PEER_TOPK, toks] = e / jnp.sum(e, axis=0, keepdims=True)
        idx_ref[toks, :] = (exp_ref[...].T * float(ROW_SUBLANES)).astype(jnp.int32)


def _route_call(hn2, wq, keys, tm):
    t, d = hn2.shape
    return pl.pallas_call(
        _route_kernel,
        out_shape=(jax.ShapeDtypeStruct((t, PEER_SLOTS), jnp.int32),
                   jax.ShapeDtypeStruct((PEER_SLOTS, t), F32)),
        grid=(t // tm,),
        in_specs=[pl.BlockSpec((tm, d), lambda i: (i, 0)),
                  pl.BlockSpec(wq.shape, lambda i: (0, 0)),
                  pl.BlockSpec(keys.shape, lambda i: (0, 0, 0))],
        out_specs=(pl.BlockSpec((tm, PEER_SLOTS), lambda i: (i, 0)),
                   pl.BlockSpec((PEER_SLOTS, tm), lambda i: (0, i))),
        scratch_shapes=[pltpu.VMEM((tm, wq.shape[1]), F32),
                        pltpu.VMEM((2, PEER_TOPK, LANES), F32),
                        pltpu.VMEM((2, PEER_TOPK, LANES), F32),
                        pltpu.VMEM((PEER_TOPK, LANES), F32),
                        pltpu.VMEM((PEER_SLOTS, LANES), F32)],
        compiler_params=pltpu.CompilerParams(vmem_limit_bytes=VMEM_LIMIT),
        name="route",
    )(hn2, wq, keys)


def _pack_kernel(t_ref, o_ref):
    half = t_ref.shape[1] // 2
    lo = pltpu.bitcast(t_ref[:, :half].astype(BF16).astype(F32), jnp.uint32)
    hi = pltpu.bitcast(t_ref[:, half:].astype(BF16).astype(F32), jnp.uint32)
    word = hi | (lo >> 16)
    rows = t_ref.shape[0]
    for s in range(ROW_SUBLANES):
        o_ref[pl.ds(s, rows, stride=ROW_SUBLANES), :] = word[:, s * LANES:(s + 1) * LANES]


def _pack_table(tbl):
    n, d = tbl.shape
    rows = 512
    return pl.pallas_call(
        _pack_kernel,
        out_shape=jax.ShapeDtypeStruct((n * ROW_SUBLANES, LANES), jnp.uint32),
        grid=(n // rows,),
        in_specs=[pl.BlockSpec((rows, d), lambda i: (i, 0))],
        out_specs=pl.BlockSpec((rows * ROW_SUBLANES, LANES), lambda i: (i, 0)),
        name="pack",
    )(tbl)


def _unpack_lo(w):
    return pltpu.bitcast(w << 16, F32)


def _unpack_hi(w):
    return pltpu.bitcast(w & jnp.uint32(0xFFFF0000), F32)


def _gather_rows(idx_ref, tbl_ref, slot_ref, t):
    for j in range(PEER_SLOTS):
        start = pl.multiple_of(idx_ref[t, j], ROW_SUBLANES)
        slot_ref[j * ROW_SUBLANES:(j + 1) * ROW_SUBLANES, :] = tbl_ref[pl.ds(start, ROW_SUBLANES), :]


def _peer_u_kernel(idx_ref, x_ref, gate_ref, tbl_ref, w_ref, slot_a, slot_b, prod_a, prod_b, x3_ref, *, tb):
    per_vreg = SUBLANES // ROW_SUBLANES
    lane = lax.broadcasted_iota(jnp.int32, (PEER_SLOTS, LANES), 1)
    for r in range(SUBLANES):
        x3_ref[:, r, :] = x_ref[:, r * LANES:(r + 1) * LANES]

    def products(slot_ref, prod_ref, t):
        xt = x3_ref[t]
        xlo = jnp.concatenate([xt[0:ROW_SUBLANES]] * per_vreg, axis=0)
        xhi = jnp.concatenate([xt[ROW_SUBLANES:]] * per_vreg, axis=0)
        words = slot_ref[...].reshape(PEER_SLOTS // per_vreg, SUBLANES, LANES)
        prod = _unpack_lo(words) * xlo[None] + _unpack_hi(words) * xhi[None]
        prod_ref[...] = prod.reshape(PEER_SLOTS * ROW_SUBLANES, LANES)

    def reduce_into(prod_ref, tl, acc):
        part = prod_ref[pl.ds(0, PEER_SLOTS, stride=ROW_SUBLANES), :]
        for r in range(1, ROW_SUBLANES):
            part = part + prod_ref[pl.ds(r, PEER_SLOTS, stride=ROW_SUBLANES), :]
        col = jnp.sum(part, axis=-1, keepdims=True)
        return jnp.where(lane == tl, col, acc)

    prod_a[...] = jnp.zeros(prod_a.shape, F32)
    prod_b[...] = jnp.zeros(prod_b.shape, F32)
    for blk in range(tb // LANES):
        base = blk * LANES
        if blk == 0:
            _gather_rows(idx_ref, tbl_ref, slot_a, base)
            _gather_rows(idx_ref, tbl_ref, slot_b, base + 1)

        def body(i, acc):
            tl = 2 * i
            acc = reduce_into(prod_a, tl - 2, acc)
            acc = reduce_into(prod_b, tl - 1, acc)
            products(slot_a, prod_a, base + tl)
            _gather_rows(idx_ref, tbl_ref, slot_a, jnp.minimum(base + tl + 2, tb - 1))
            products(slot_b, prod_b, base + tl + 1)
            _gather_rows(idx_ref, tbl_ref, slot_b, jnp.minimum(base + tl + 3, tb - 1))
            return acc

        act = lax.fori_loop(0, LANES // 2, body, jnp.zeros((PEER_SLOTS, LANES), F32))
        act = reduce_into(prod_a, LANES - 2, act)
        act = reduce_into(prod_b, LANES - 1, act)
        gelu = 0.5 * act * (1.0 + lax.erf(act * (1.0 / math.sqrt(2.0))))
        w_ref[blk * LANES:(blk + 1) * LANES, :] = (gate_ref[:, blk * LANES:(blk + 1) * LANES] * gelu).T


def _peer_v_kernel(idx_ref, w_ref, tbl_ref, h1_ref, mod_ref, g_ref, o_ref,
                   slot_a, slot_b, wrep_hi, wrep_lo, o3_ref, *, tb):
    cols = 2 * ROW_SUBLANES * PEER_SLOTS
    w = w_ref[...]
    hi = w.astype(BF16)
    lo = (w - hi.astype(F32)).astype(BF16)
    jr = lax.broadcasted_iota(jnp.int32, (PEER_SLOTS, cols), 0)
    jc = lax.broadcasted_iota(jnp.int32, (PEER_SLOTS, cols), 1)
    expand = jnp.where(jc // (2 * ROW_SUBLANES) == jr, 1.0, 0.0).astype(BF16)
    wrep_hi[...] = jnp.dot(hi, expand, preferred_element_type=F32)
    wrep_lo[...] = jnp.dot(lo, expand, preferred_element_type=F32)
    rr = lax.broadcasted_iota(jnp.int32, (SUBLANES, cols), 0)
    rc = lax.broadcasted_iota(jnp.int32, (SUBLANES, cols), 1)
    mask = (rc % (2 * ROW_SUBLANES)) == 2 * (rr % ROW_SUBLANES) + rr // ROW_SUBLANES

    def lhs_rows(t):
        return [jnp.where(mask, jnp.broadcast_to(ref[pl.ds(t, 1), :], (SUBLANES, cols)), 0.0)
                for ref in (wrep_hi, wrep_lo)]

    def gather_pair(slot_ref, t):
        for k in range(2):
            for j in range(PEER_SLOTS):
                start = pl.multiple_of(idx_ref[t + k, j], ROW_SUBLANES)
                slot_ref[j * ROW_SUBLANES:(j + 1) * ROW_SUBLANES, k * LANES:(k + 1) * LANES] = (
                    tbl_ref[pl.ds(start, ROW_SUBLANES), :])

    def store_token(t, val):
        o3_ref[t] = val

    def combine_pair(slot_ref, t):
        lhs = jnp.concatenate(lhs_rows(t) + lhs_rows(t + 1), axis=0).astype(BF16)
        res = jnp.dot(lhs, pltpu.bitcast(slot_ref[...], BF16), preferred_element_type=F32)
        store_token(t, res[0:SUBLANES, 0:LANES] + res[SUBLANES:2 * SUBLANES, 0:LANES])
        store_token(t + 1, res[2 * SUBLANES:3 * SUBLANES, LANES:] + res[3 * SUBLANES:, LANES:])

    pairs = slot_a.shape[0]
    half = 2 * pairs
    for q in range(pairs):
        gather_pair(slot_a.at[q], 2 * q)
        gather_pair(slot_b.at[q], half + 2 * q)

    def body(i, carry):
        t0 = 2 * half * i
        for q in range(pairs):
            combine_pair(slot_a.at[q], t0 + 2 * q)
        for q in range(pairs):
            gather_pair(slot_a.at[q], jnp.minimum(t0 + 2 * half + 2 * q, tb - 2))
        for q in range(pairs):
            combine_pair(slot_b.at[q], t0 + half + 2 * q)
        for q in range(pairs):
            gather_pair(slot_b.at[q], jnp.minimum(t0 + 3 * half + 2 * q, tb - 2))
        return carry

    lax.fori_loop(0, tb // (2 * half), body, 0)
    for r in range(SUBLANES):
        cols = slice(r * LANES, (r + 1) * LANES)
        o_ref[:, cols] = h1_ref[:, cols] + mod_ref[0, 5:6, cols] * o3_ref[:, r, :]
    h = o_ref[...]
    o_ref[...] = h * lax.rsqrt(jnp.mean(h * h, axis=-1, keepdims=True) + NORM_EPS) * g_ref[...]


def _table_spec(tbl):
    return pl.BlockSpec(tbl.shape, lambda i: (0, 0), pipeline_mode=pl.Buffered(1))


def _peer_u_call(idx_t, x2, gate_t, tbl, tb):
    t, d = x2.shape
    return pl.pallas_call(
        functools.partial(_peer_u_kernel, tb=tb),
        out_shape=jax.ShapeDtypeStruct((t, PEER_SLOTS), F32),
        grid=(t // tb,),
        in_specs=[pl.BlockSpec((tb, PEER_SLOTS), lambda i: (i, 0), memory_space=pltpu.SMEM),
                  pl.BlockSpec((tb, d), lambda i: (i, 0)),
                  pl.BlockSpec((PEER_SLOTS, tb), lambda i: (0, i)),
                  _table_spec(tbl)],
        out_specs=pl.BlockSpec((tb, PEER_SLOTS), lambda i: (i, 0)),
        scratch_shapes=[pltpu.VMEM((PEER_SLOTS * ROW_SUBLANES, LANES), jnp.uint32),
                        pltpu.VMEM((PEER_SLOTS * ROW_SUBLANES, LANES), jnp.uint32),
                        pltpu.VMEM((PEER_SLOTS * ROW_SUBLANES, LANES), F32),
                        pltpu.VMEM((PEER_SLOTS * ROW_SUBLANES, LANES), F32),
                        pltpu.VMEM((tb, SUBLANES, LANES), F32)],
        compiler_params=pltpu.CompilerParams(vmem_limit_bytes=VMEM_LIMIT),
        name="peer_u",
    )(idx_t, x2, gate_t, tbl)


def _peer_v_call(idx_t, w_t, tbl, h1, mod3, g, seq, tb):
    t = idx_t.shape[0]
    per_b = seq // tb
    return pl.pallas_call(
        functools.partial(_peer_v_kernel, tb=tb),
        out_shape=jax.ShapeDtypeStruct((t, D_MODEL), F32),
        grid=(t // tb,),
        in_specs=[pl.BlockSpec((tb, PEER_SLOTS), lambda i: (i, 0), memory_space=pltpu.SMEM),
                  pl.BlockSpec((tb, PEER_SLOTS), lambda i: (i, 0)),
                  _table_spec(tbl),
                  pl.BlockSpec((tb, D_MODEL), lambda i: (i, 0)),
                  pl.BlockSpec((1, 6, D_MODEL), lambda i: (i // per_b, 0, 0)),
                  pl.BlockSpec((1, D_MODEL), lambda i: (0, 0))],
        out_specs=pl.BlockSpec((tb, D_MODEL), lambda i: (i, 0)),
        scratch_shapes=[pltpu.VMEM((2, PEER_SLOTS * ROW_SUBLANES, 2 * LANES), jnp.uint32),
                        pltpu.VMEM((2, PEER_SLOTS * ROW_SUBLANES, 2 * LANES), jnp.uint32),
                        pltpu.VMEM((tb, 2 * ROW_SUBLANES * PEER_SLOTS), F32),
                        pltpu.VMEM((tb, 2 * ROW_SUBLANES * PEER_SLOTS), F32),
                        pltpu.VMEM((tb, SUBLANES, LANES), F32)],
        compiler_params=pltpu.CompilerParams(vmem_limit_bytes=VMEM_LIMIT),
        name="peer_v",
    )(idx_t, w_t, tbl, h1, mod3, g.reshape(1, D_MODEL))


def _pad_lanes(v):
    return jnp.pad(v.reshape(1, -1), ((0, 0), (0, LANES - v.shape[-1])))


def kernel(x, c, ada_w, ada_b, norm1_g, w_in, ssd_conv_w, ssd_conv_b, ssd_dt_bias, ssd_a_log, ssd_d, ssd_norm_g, conf_dw_w, conf_dw_b, conf_ln_g, conf_ln_b, w_out, norm2_g, peer_w_query, peer_sub_keys, peer_u, peer_v, final_norm_g):
    bsz, seq, d = x.shape
    assert d == D_MODEL and ada_w.shape[0] == 1
    t = bsz * seq
    tm = min(512, seq)
    ts = min(256, seq)
    tr = min(256, seq)
    tb = min(512, seq)
    x2 = x.reshape(t, d)

    mod3 = _mod_call(c, ada_w[0], ada_b[0]).reshape(bsz, 6, d)

    wi = w_in[0]
    o1 = SSD_WIDTH
    o2 = o1 + SSD_XBC
    o3 = o2 + SSD_HEADS
    wz = wi[:, :o1].astype(BF16)
    wx = wi[:, o1:o2].astype(BF16)
    wd = jnp.pad(wi[:, o2:o3], ((0, 0), (0, LANES - SSD_HEADS))).astype(BF16)
    wg = wi[:, o3:].astype(BF16)
    z, xbc, glu, dt = _inproj_call(x2, mod3, norm1_g[0], wz, wx, wg, wd, seq, tm)

    h1, hn2 = _mixer_call(
        x2, z, xbc, glu, dt, mod3,
        ssd_conv_w[0], ssd_conv_b[0].reshape(1, -1), _pad_lanes(ssd_dt_bias[0]), _pad_lanes(ssd_a_log[0]),
        jnp.repeat(ssd_d[0], SSD_HEAD_DIM).reshape(1, -1), ssd_norm_g[0].reshape(1, -1),
        conf_dw_w[0], conf_dw_b[0].reshape(1, -1), conf_ln_g[0].reshape(1, -1), conf_ln_b[0].reshape(1, -1),
        w_out[0].astype(BF16), norm2_g[0].reshape(1, -1), bsz, seq, ts)

    keys = peer_sub_keys[0].reshape(PEER_HEADS * 2, PEER_N_KEYS, PEER_D_HALF).astype(BF16)
    idx_t, gate_t = _route_call(hn2, peer_w_query[0].astype(BF16), keys, tr)

    w_t = _peer_u_call(idx_t, hn2, gate_t, _pack_table(peer_u[0]), tb)
    out = _peer_v_call(idx_t, w_t, _pack_table(peer_v[0]), h1, mod3, final_norm_g, seq, tb)
    return out.reshape(bsz, seq, d)
```

```python
import functools
import math

import jax
import jax.numpy as jnp
from jax import lax
from jax.experimental import pallas as pl
from jax.experimental.pallas import tpu as pltpu

F32 = jnp.float32
BF16 = jnp.bfloat16
HIGHEST = lax.Precision.HIGHEST

D_MODEL = 1024
CHUNK = 64
SSD_WIDTH = 512
SSD_HEADS = 8
SSD_HEAD_DIM = 64
SSD_GROUPS = 2
SSD_STATE = 128
SSD_CONV = 4
SSD_XBC = 1024
CONF_WIDTH = 512
CONF_CONV = 31
PEER_HEADS = 8
PEER_N_KEYS = 128
PEER_D_HALF = 128
PEER_TOPK = 16
PEER_SLOTS = PEER_HEADS * PEER_TOPK
NORM_EPS = 1e-6

LANES = 128
SUBLANES = 8
ROW_SUBLANES = D_MODEL // 2 // LANES
VMEM_LIMIT = 56 * 1024 * 1024

V_SPLIT = 8
U_SPLIT = 2

XBC_TAIL = 8
GLU_TAIL = 32


def _silu(v):
    return v * jax.nn.sigmoid(v)


def _softplus(v):
    return jnp.maximum(v, 0.0) + jnp.log(1.0 + jnp.exp(-jnp.abs(v)))


def _bdot(a, b):
    return jnp.dot(a.astype(BF16), b.astype(BF16), preferred_element_type=F32)


def _mod_kernel(c_ref, w_ref, b_ref, o_ref):
    cond = _silu(c_ref[...])
    o_ref[...] = jnp.dot(cond, w_ref[...], precision=HIGHEST, preferred_element_type=F32) + b_ref[...]


def _mod_call(c, ada_w, ada_b):
    bsz, d = c.shape
    n = ada_w.shape[1]
    return pl.pallas_call(
        _mod_kernel,
        out_shape=jax.ShapeDtypeStruct((bsz, n), F32),
        grid=(n // d,),
        in_specs=[pl.BlockSpec((bsz, d), lambda i: (0, 0)),
                  pl.BlockSpec((d, d), lambda i: (0, i)),
                  pl.BlockSpec((1, d), lambda i: (0, i))],
        out_specs=pl.BlockSpec((bsz, d), lambda i: (0, i)),
        name="mod",
    )(c, ada_w, ada_b.reshape(1, n))


def _inproj_kernel(x_ref, mod_ref, g_ref, wz_ref, wx_ref, wg_ref, wd_ref,
                   z_ref, xbc_ref, glu_ref, dt_ref):
    x = x_ref[...]
    ms = jnp.mean(x * x, axis=-1, keepdims=True)
    y = x * lax.rsqrt(ms + NORM_EPS) * g_ref[...]
    sh = mod_ref[0, 0:1, :]
    sc = mod_ref[0, 1:2, :]
    hn = (y * (1.0 + sc) + sh).astype(BF16)
    z_ref[...] = jnp.dot(hn, wz_ref[...], preferred_element_type=F32)
    xbc_ref[...] = jnp.dot(hn, wx_ref[...], preferred_element_type=F32)
    glu_ref[...] = jnp.dot(hn, wg_ref[...], preferred_element_type=F32)
    dt_ref[...] = jnp.dot(hn, wd_ref[...], preferred_element_type=F32)


def _inproj_call(x2, mod3, norm1_g, wz, wx, wg, wd, seq, tm):
    t, d = x2.shape
    per_b = seq // tm
    const = lambda i: (0, 0)
    row = lambda i: (i, 0)
    return pl.pallas_call(
        _inproj_kernel,
        out_shape=(jax.ShapeDtypeStruct((t, SSD_WIDTH), F32),
                   jax.ShapeDtypeStruct((t, SSD_XBC), F32),
                   jax.ShapeDtypeStruct((t, 2 * CONF_WIDTH), F32),
                   jax.ShapeDtypeStruct((t, LANES), F32)),
        grid=(t // tm,),
        in_specs=[pl.BlockSpec((tm, d), row),
                  pl.BlockSpec((1, 6, d), lambda i: (i // per_b, 0, 0)),
                  pl.BlockSpec((1, d), const),
                  pl.BlockSpec(wz.shape, const),
                  pl.BlockSpec(wx.shape, const),
                  pl.BlockSpec(wg.shape, const),
                  pl.BlockSpec(wd.shape, const)],
        out_specs=(pl.BlockSpec((tm, SSD_WIDTH), row),
                   pl.BlockSpec((tm, SSD_XBC), row),
                   pl.BlockSpec((tm, 2 * CONF_WIDTH), row),
                   pl.BlockSpec((tm, LANES), row)),
        compiler_params=pltpu.CompilerParams(vmem_limit_bytes=VMEM_LIMIT),
        name="inproj",
    )(x2, mod3, norm1_g.reshape(1, d), wz, wx, wg, wd)


def _mixer_kernel(x_ref, z_ref, xbc_ref, glu_ref, dt_ref, mod_ref,
                  cw_ref, cb_ref, dtb_ref, alog_ref, dexp_ref, sng_ref,
                  dww_ref, dwb_ref, lng_ref, lnb_ref, wout_ref, n2g_ref,
                  h1_ref, hn2_ref,
                  xext_ref, gext_ref, hst_ref, xc_ref, xdt_ref, acs_ref, eacs_ref,
                  acst_ref, bmt_ref, y_ref, gsh_ref, *, ts):
    nc = ts // CHUNK
    hw = SSD_WIDTH // SSD_GROUPS

    @pl.when(pl.program_id(1) == 0)
    def _():
        xext_ref[0:XBC_TAIL, :] = jnp.zeros((XBC_TAIL, SSD_XBC), F32)
        gext_ref[0:GLU_TAIL, :] = jnp.zeros((GLU_TAIL, CONF_WIDTH), F32)
        hst_ref[...] = jnp.zeros(hst_ref.shape, F32)

    xext_ref[XBC_TAIL:XBC_TAIL + ts, :] = xbc_ref[...]
    acc = cb_ref[...] + cw_ref[0:1, :] * xext_ref[pl.ds(XBC_TAIL - SSD_CONV + 1, ts), :]
    for k in range(1, SSD_CONV):
        acc = acc + cw_ref[k:k + 1, :] * xext_ref[pl.ds(XBC_TAIL - SSD_CONV + 1 + k, ts), :]
    xext_ref[0:XBC_TAIL, :] = xext_ref[ts:ts + XBC_TAIL, :]
    xc_ref[...] = _silu(acc)

    dt = _softplus(dt_ref[...] + dtb_ref[...])
    dta = dt * (-jnp.exp(alog_ref[...]))
    ri = lax.broadcasted_iota(jnp.int32, (ts, ts), 0)
    ci = lax.broadcasted_iota(jnp.int32, (ts, ts), 1)
    ltri = jnp.where(((ri // CHUNK) == (ci // CHUNK)) & (ci <= ri), 1.0, 0.0).astype(F32)
    acs = jnp.dot(ltri, dta, precision=HIGHEST, preferred_element_type=F32)
    acst_ref[...] = acs.T
    er = lax.broadcasted_iota(jnp.int32, (LANES, SSD_WIDTH), 0)
    ec = lax.broadcasted_iota(jnp.int32, (LANES, SSD_WIDTH), 1)
    expand = jnp.where((ec // SSD_HEAD_DIM) == er, 1.0, 0.0).astype(F32)
    dt_exp = jnp.dot(dt, expand, precision=HIGHEST, preferred_element_type=F32)
    acs_exp = jnp.dot(acs, expand, precision=HIGHEST, preferred_element_type=F32)
    acs_ref[...] = acs_exp
    eacs_ref[...] = jnp.exp(acs_exp)
    xdt_ref[...] = xc_ref[:, 0:SSD_WIDTH] * dt_exp
    bmt_ref[...] = xc_ref[:, SSD_WIDTH:SSD_WIDTH + SSD_GROUPS * SSD_STATE].T

    tr = lax.broadcasted_iota(jnp.int32, (CHUNK, CHUNK), 0)
    tc = lax.broadcasted_iota(jnp.int32, (CHUNK, CHUNK), 1)
    tril = tc <= tr
    c_off = SSD_WIDTH + SSD_GROUPS * SSD_STATE

    for c in range(nc):
        r0 = c * CHUNK
        rows = slice(r0, r0 + CHUNK)
        a_last = acs_ref[r0 + CHUNK - 1:r0 + CHUNK, :]
        xw = xdt_ref[rows, :] * jnp.exp(a_last - acs_ref[rows, :])
        cdec = jnp.exp(a_last)
        y_parts = []
        for g in range(SSD_GROUPS):
            cg = xc_ref[rows, c_off + g * SSD_STATE:c_off + (g + 1) * SSD_STATE].astype(BF16)
            bg = xc_ref[rows, SSD_WIDTH + g * SSD_STATE:SSD_WIDTH + (g + 1) * SSD_STATE].astype(BF16)
            cb = lax.dot_general(cg, bg, (((1,), (1,)), ((), ())), preferred_element_type=F32)
            hg = hst_ref[g]
            yoff = jnp.dot(cg, hg.astype(BF16), preferred_element_type=F32)
            st = _bdot(bmt_ref[g * SSD_STATE:(g + 1) * SSD_STATE, rows], xw[:, g * hw:(g + 1) * hw])
            hst_ref[g] = hg * cdec[:, g * hw:(g + 1) * hw] + st
            yds = []
            for hh in range(SSD_HEADS // SSD_GROUPS):
                h = g * (SSD_HEADS // SSD_GROUPS) + hh
                cols = slice(h * SSD_HEAD_DIM, (h + 1) * SSD_HEAD_DIM)
                seg = acs_ref[rows, cols] - acst_ref[h:h + 1, rows]
                dec = jnp.exp(jnp.where(tril, seg, -jnp.inf))
                yds.append(_bdot(cb * dec, xdt_ref[rows, cols]))
            y_parts.append(jnp.concatenate(yds, axis=1) + yoff * eacs_ref[rows, g * hw:(g + 1) * hw])
        y_ref[rows, :] = jnp.concatenate(y_parts, axis=1) + dexp_ref[...] * xc_ref[rows, 0:SSD_WIDTH]

    y = y_ref[...] * _silu(z_ref[...])
    y_ssd = y * lax.rsqrt(jnp.mean(y * y, axis=-1, keepdims=True) + NORM_EPS) * sng_ref[...]

    gext_ref[GLU_TAIL:GLU_TAIL + ts, :] = glu_ref[:, 0:CONF_WIDTH] * jax.nn.sigmoid(glu_ref[:, CONF_WIDTH:])
    span = ts + GLU_TAIL - SUBLANES
    for s in range(1, SUBLANES):
        gsh_ref[s - 1, 0:span, :] = gext_ref[pl.ds(s, span), :]
    u = dwb_ref[...]
    for k in range(CONF_CONV):
        off = GLU_TAIL - CONF_CONV + 1 + k
        s = off % SUBLANES
        rows = pl.ds(off - s, ts)
        u = u + dww_ref[k:k + 1, :] * (gext_ref[rows, :] if s == 0 else gsh_ref[s - 1, rows, :])
    gext_ref[0:GLU_TAIL, :] = gext_ref[ts:ts + GLU_TAIL, :]
    mu = jnp.mean(u, axis=-1, keepdims=True)
    uc = u - mu
    var = jnp.mean(uc * uc, axis=-1, keepdims=True)
    y_conf = _silu(uc * lax.rsqrt(var + NORM_EPS) * lng_ref[...] + lnb_ref[...])

    mix = (jnp.dot(y_ssd.astype(BF16), wout_ref[0:SSD_WIDTH, :], preferred_element_type=F32)
           + jnp.dot(y_conf.astype(BF16), wout_ref[SSD_WIDTH:, :], preferred_element_type=F32))
    h1 = x_ref[...] + mod_ref[0, 2:3, :] * mix
    h1_ref[...] = h1
    hn = h1 * lax.rsqrt(jnp.mean(h1 * h1, axis=-1, keepdims=True) + NORM_EPS) * n2g_ref[...]
    hn2_ref[...] = hn * (1.0 + mod_ref[0, 4:5, :]) + mod_ref[0, 3:4, :]


def _mixer_call(x2, z, xbc, glu, dt, mod3, cw, cb, dtb, alog, dexp, sng, dww, dwb, lng, lnb,
                wout, n2g, bsz, seq, ts):
    t, d = x2.shape
    per_b = seq // ts
    row = lambda b, j: (b * per_b + j, 0)
    const = lambda b, j: (0, 0)

    def full(a):
        return pl.BlockSpec(a.shape, const)

    return pl.pallas_call(
        functools.partial(_mixer_kernel, ts=ts),
        out_shape=(jax.ShapeDtypeStruct((t, d), F32), jax.ShapeDtypeStruct((t, d), F32)),
        grid=(bsz, per_b),
        in_specs=[pl.BlockSpec((ts, d), row),
                  pl.BlockSpec((ts, SSD_WIDTH), row),
                  pl.BlockSpec((ts, SSD_XBC), row),
                  pl.BlockSpec((ts, 2 * CONF_WIDTH), row),
                  pl.BlockSpec((ts, LANES), row),
                  pl.BlockSpec((1, 6, d), lambda b, j: (b, 0, 0)),
                  full(cw), full(cb), full(dtb), full(alog), full(dexp), full(sng),
                  full(dww), full(dwb), full(lng), full(lnb), full(wout), full(n2g)],
        out_specs=(pl.BlockSpec((ts, d), row), pl.BlockSpec((ts, d), row)),
        scratch_shapes=[pltpu.VMEM((ts + XBC_TAIL, SSD_XBC), F32),
                        pltpu.VMEM((ts + GLU_TAIL, CONF_WIDTH), F32),
                        pltpu.VMEM((SSD_GROUPS, SSD_STATE, SSD_WIDTH // SSD_GROUPS), F32),
                        pltpu.VMEM((ts, SSD_XBC), F32),
                        pltpu.VMEM((ts, SSD_WIDTH), F32),
                        pltpu.VMEM((ts, SSD_WIDTH), F32),
                        pltpu.VMEM((ts, SSD_WIDTH), F32),
                        pltpu.VMEM((LANES, ts), F32),
                        pltpu.VMEM((SSD_GROUPS * SSD_STATE, ts), F32),
                        pltpu.VMEM((ts, SSD_WIDTH), F32),
                        pltpu.VMEM((SUBLANES - 1, ts + GLU_TAIL - SUBLANES, CONF_WIDTH), F32)],
        compiler_params=pltpu.CompilerParams(
            dimension_semantics=("arbitrary", "arbitrary"), vmem_limit_bytes=VMEM_LIMIT),
        name="mixer",
    )(x2, z, xbc, glu, dt, mod3, cw, cb, dtb, alog, dexp, sng, dww, dwb, lng, lnb, wout, n2g)


_PAIR_COUNTS = tuple(PEER_TOPK // (a + 1) for a in range(PEER_TOPK))


def _topk_rows(s, val_ref, pick_ref, payload=None):
    n = s.shape[0]
    h = n // 2
    iota = lax.broadcasted_iota(jnp.int32, (h, s.shape[1]), 0).astype(F32)
    a, b = s[:h], s[h:]
    first = a >= b
    hi, lo = jnp.where(first, a, b), jnp.where(first, b, a)
    ihi, ilo = jnp.where(first, iota, iota + float(h)), jnp.where(first, iota + float(h), iota)
    if payload is not None:
        phi, plo = jnp.where(first, payload[:h], payload[h:]), jnp.where(first, payload[h:], payload[:h])
    for r in range(val_ref.shape[0]):
        m = jnp.max(hi, axis=0, keepdims=True)
        am = jnp.min(jnp.where(hi == m, ihi, float(n)), axis=0, keepdims=True)
        val_ref[r:r + 1, :] = m
        hit = ihi == am
        if payload is None:
            pick_ref[r:r + 1, :] = am
        else:
            pick_ref[r:r + 1, :] = jnp.max(jnp.where(hit, phi, -1.0), axis=0, keepdims=True)
            phi = jnp.where(hit, plo, phi)
        hi = jnp.where(hit, lo, hi)
        ihi = jnp.where(hit, ilo, ihi)
        lo = jnp.where(hit, -jnp.inf, lo)


def _route_kernel(hn_ref, wq_ref, keys_ref, idx_ref, gate_ref, q_ref, topv_ref, topi_ref, best_ref, exp_ref):
    q_ref[...] = jnp.dot(hn_ref[...].astype(BF16), wq_ref[...], preferred_element_type=F32)
    nt = (((1,), (1,)), ((), ()))
    n_cand = sum(_PAIR_COUNTS)
    n_pad = -n_cand % (2 * SUBLANES)
    for lt in range(q_ref.shape[0] // LANES):
        toks = slice(lt * LANES, (lt + 1) * LANES)
        for h in range(PEER_HEADS):
            for i in range(2):
                col = (h * 2 + i) * PEER_D_HALF
                qh = q_ref[toks, col:col + PEER_D_HALF].astype(BF16)
                sc = lax.dot_general(keys_ref[h * 2 + i], qh, nt, preferred_element_type=F32)
                _topk_rows(sc, topv_ref.at[i], topi_ref.at[i])
            sv1, sv2 = topv_ref[0], topv_ref[1]
            si1, si2 = topi_ref[0], topi_ref[1]
            cand = jnp.concatenate([sv1[a:a + 1] + sv2[0:nb] for a, nb in enumerate(_PAIR_COUNTS)]
                                   + [jnp.full((n_pad, LANES), -jnp.inf, F32)], axis=0)
            cidx = jnp.concatenate([si1[a:a + 1] * float(PEER_N_KEYS) + si2[0:nb] for a, nb in enumerate(_PAIR_COUNTS)]
                                   + [jnp.zeros((n_pad, LANES), F32)], axis=0)
            _topk_rows(cand, best_ref, exp_ref.at[pl.ds(h * PEER_TOPK, PEER_TOPK)], payload=cidx)
            best = best_ref[...]
            e = jnp.exp(best - best[0:1, :])
            gate_ref[h * PEER_TOPK:(h + 1) * PEER_TOPK, toks] = e / jnp.sum(e, axis=0, keepdims=True)
        idx_ref[toks, :] = (exp_ref[...].T * float(ROW_SUBLANES)).astype(jnp.int32)


def _route_call(hn2, wq, keys, tm):
    t, d = hn2.shape
    return pl.pallas_call(
        _route_kernel,
        out_shape=(jax.ShapeDtypeStruct((t, PEER_SLOTS), jnp.int32),
                   jax.ShapeDtypeStruct((PEER_SLOTS, t), F32)),
        grid=(t // tm,),
        in_specs=[pl.BlockSpec((tm, d), lambda i: (i, 0)),
                  pl.BlockSpec(wq.shape, lambda i: (0, 0)),
                  pl.BlockSpec(keys.shape, lambda i: (0, 0, 0))],
        out_specs=(pl.BlockSpec((tm, PEER_SLOTS), lambda i: (i, 0)),
                   pl.BlockSpec((PEER_SLOTS, tm), lambda i: (0, i))),
        scratch_shapes=[pltpu.VMEM((tm, wq.shape[1]), F32),
                        pltpu.VMEM((2, PEER_TOPK, LANES), F32),
                        pltpu.VMEM((2, PEER_TOPK, LANES), F32),
                        pltpu.VMEM((PEER_TOPK, LANES), F32),
                        pltpu.VMEM((PEER_SLOTS, LANES), F32)],
        compiler_params=pltpu.CompilerParams(vmem_limit_bytes=VMEM_LIMIT),
        name="route",
    )(hn2, wq, keys)


def _pack_kernel(t_ref, o_ref):
    half = t_ref.shape[1] // 2
    lo = pltpu.bitcast(t_ref[:, :half].astype(BF16).astype(F32), jnp.uint32)
    hi = pltpu.bitcast(t_ref[:, half:].astype(BF16).astype(F32), jnp.uint32)
    word = hi | (lo >> 16)
    rows = t_ref.shape[0]
    for s in range(ROW_SUBLANES):
        o_ref[pl.ds(s, rows, stride=ROW_SUBLANES), :] = word[:, s * LANES:(s + 1) * LANES]


def _pack_table(tbl):
    n, d = tbl.shape
    rows = 512
    return pl.pallas_call(
        _pack_kernel,
        out_shape=jax.ShapeDtypeStruct((n * ROW_SUBLANES, LANES), jnp.uint32),
        grid=(n // rows,),
        in_specs=[pl.BlockSpec((rows, d), lambda i: (i, 0))],
        out_specs=pl.BlockSpec((rows * ROW_SUBLANES, LANES), lambda i: (i, 0)),
        name="pack",
    )(tbl)


def _unpack_lo(w):
    return pltpu.bitcast(w << 16, F32)


def _unpack_hi(w):
    return pltpu.bitcast(w & jnp.uint32(0xFFFF0000), F32)


def _split_tokens(idx_t, ways):
    t, n = idx_t.shape
    parts = idx_t.reshape(t // ways, ways, n)
    return [parts[:, k, :] for k in range(ways)]


def _gather_rows(idx_refs, tbl_ref, dsts, p):
    for j in range(PEER_SLOTS):
        for idx_ref, (slot_ref, lane0) in zip(idx_refs, dsts):
            start = pl.multiple_of(idx_ref[p, j], ROW_SUBLANES)
            slot_ref[j * ROW_SUBLANES:(j + 1) * ROW_SUBLANES, lane0:lane0 + LANES] = (
                tbl_ref[pl.ds(start, ROW_SUBLANES), :])


def _peer_u_kernel(*refs, tb):
    idx_refs = refs[:U_SPLIT]
    x_ref, gate_ref, tbl_ref, w_ref, slot_ref, prod_ref, x3_ref = refs[U_SPLIT:]
    per_vreg = SUBLANES // ROW_SUBLANES
    lane = lax.broadcasted_iota(jnp.int32, (PEER_SLOTS, LANES), 1)
    for r in range(SUBLANES):
        x3_ref[:, r, :] = x_ref[:, r * LANES:(r + 1) * LANES]

    def products(slot_ref, prod_ref, t):
        xt = x3_ref[t]
        xlo = jnp.concatenate([xt[0:ROW_SUBLANES]] * per_vreg, axis=0)
        xhi = jnp.concatenate([xt[ROW_SUBLANES:]] * per_vreg, axis=0)
        words = slot_ref[...].reshape(PEER_SLOTS // per_vreg, SUBLANES, LANES)
        prod = _unpack_lo(words) * xlo[None] + _unpack_hi(words) * xhi[None]
        prod_ref[...] = prod.reshape(PEER_SLOTS * ROW_SUBLANES, LANES)

    def reduce_into(prod_ref, tl, acc):
        part = prod_ref[pl.ds(0, PEER_SLOTS, stride=ROW_SUBLANES), :]
        for r in range(1, ROW_SUBLANES):
            part = part + prod_ref[pl.ds(r, PEER_SLOTS, stride=ROW_SUBLANES), :]
        col = jnp.sum(part, axis=-1, keepdims=True)
        return jnp.where(lane == tl, col, acc)

    n = U_SPLIT
    prod_ref[...] = jnp.zeros(prod_ref.shape, F32)
    dsts = [(slot_ref.at[k], 0) for k in range(n)]
    for blk in range(tb // LANES):
        base = blk * LANES
        if blk == 0:
            _gather_rows(idx_refs, tbl_ref, dsts, 0)

        def body(i, acc):
            for k in range(n):
                acc = reduce_into(prod_ref.at[k], n * (i - 1) + k, acc)
            for k in range(n):
                products(slot_ref.at[k], prod_ref.at[k], base + n * i + k)
            _gather_rows(idx_refs, tbl_ref, dsts, jnp.minimum(base // n + i + 1, tb // n - 1))
            return acc

        act = lax.fori_loop(0, LANES // n, body, jnp.zeros((PEER_SLOTS, LANES), F32))
        for k in range(n):
            act = reduce_into(prod_ref.at[k], LANES - n + k, act)
        gelu = 0.5 * act * (1.0 + lax.erf(act * (1.0 / math.sqrt(2.0))))
        w_ref[blk * LANES:(blk + 1) * LANES, :] = (gate_ref[:, blk * LANES:(blk + 1) * LANES] * gelu).T


def _peer_v_kernel(*refs, tb):
    idx_refs = refs[:V_SPLIT]
    w_ref, tbl_ref, h1_ref, mod_ref, g_ref, o_ref, slot_a, slot_b, wrep_hi, wrep_lo, o3_ref = refs[V_SPLIT:]
    cols = 2 * ROW_SUBLANES * PEER_SLOTS
    w = w_ref[...]
    hi = w.astype(BF16)
    lo = (w - hi.astype(F32)).astype(BF16)
    jr = lax.broadcasted_iota(jnp.int32, (PEER_SLOTS, cols), 0)
    jc = lax.broadcasted_iota(jnp.int32, (PEER_SLOTS, cols), 1)
    expand = jnp.where(jc // (2 * ROW_SUBLANES) == jr, 1.0, 0.0).astype(BF16)
    wrep_hi[...] = jnp.dot(hi, expand, preferred_element_type=F32)
    wrep_lo[...] = jnp.dot(lo, expand, preferred_element_type=F32)
    rr = lax.broadcasted_iota(jnp.int32, (SUBLANES, cols), 0)
    rc = lax.broadcasted_iota(jnp.int32, (SUBLANES, cols), 1)
    mask = (rc % (2 * ROW_SUBLANES)) == 2 * (rr % ROW_SUBLANES) + rr // ROW_SUBLANES

    def lhs_rows(t):
        return [jnp.where(mask, jnp.broadcast_to(ref[pl.ds(t, 1), :], (SUBLANES, cols)), 0.0)
                for ref in (wrep_hi, wrep_lo)]

    def store_token(t, val):
        o3_ref[t] = val

    def combine_pair(slot_ref, t):
        lhs = jnp.concatenate(lhs_rows(t) + lhs_rows(t + 1), axis=0).astype(BF16)
        res = jnp.dot(lhs, pltpu.bitcast(slot_ref[...], BF16), preferred_element_type=F32)
        store_token(t, res[0:SUBLANES, 0:LANES] + res[SUBLANES:2 * SUBLANES, 0:LANES])
        store_token(t + 1, res[2 * SUBLANES:3 * SUBLANES, LANES:] + res[3 * SUBLANES:, LANES:])

    pairs = slot_a.shape[0]
    half = 2 * pairs
    per_trip = 2 * half
    assert per_trip == len(idx_refs)
    dsts = [((slot_a, slot_b)[k // half].at[(k // 2) % pairs], (k % 2) * LANES) for k in range(per_trip)]
    _gather_rows(idx_refs, tbl_ref, dsts, 0)

    def body(i, carry):
        t0 = per_trip * i
        for q in range(pairs):
            combine_pair(slot_a.at[q], t0 + 2 * q)
        for q in range(pairs):
            combine_pair(slot_b.at[q], t0 + half + 2 * q)
        _gather_rows(idx_refs, tbl_ref, dsts, jnp.minimum(i + 1, tb // per_trip - 1))
        return carry

    lax.fori_loop(0, tb // per_trip, body, 0)
    for r in range(SUBLANES):
        cols = slice(r * LANES, (r + 1) * LANES)
        o_ref[:, cols] = h1_ref[:, cols] + mod_ref[0, 5:6, cols] * o3_ref[:, r, :]
    h = o_ref[...]
    o_ref[...] = h * lax.rsqrt(jnp.mean(h * h, axis=-1, keepdims=True) + NORM_EPS) * g_ref[...]


def _table_spec(tbl):
    return pl.BlockSpec(tbl.shape, lambda i: (0, 0), pipeline_mode=pl.Buffered(1))


def _peer_u_call(idx_parts, x2, gate_t, tbl, tb):
    t, d = x2.shape
    return pl.pallas_call(
        functools.partial(_peer_u_kernel, tb=tb),
        out_shape=jax.ShapeDtypeStruct((t, PEER_SLOTS), F32),
        grid=(t // tb,),
        in_specs=[pl.BlockSpec((tb // U_SPLIT, PEER_SLOTS), lambda i: (i, 0), memory_space=pltpu.SMEM)] * U_SPLIT
                 + [pl.BlockSpec((tb, d), lambda i: (i, 0)),
                    pl.BlockSpec((PEER_SLOTS, tb), lambda i: (0, i)),
                    _table_spec(tbl)],
        out_specs=pl.BlockSpec((tb, PEER_SLOTS), lambda i: (i, 0)),
        scratch_shapes=[pltpu.VMEM((U_SPLIT, PEER_SLOTS * ROW_SUBLANES, LANES), jnp.uint32),
                        pltpu.VMEM((U_SPLIT, PEER_SLOTS * ROW_SUBLANES, LANES), F32),
                        pltpu.VMEM((tb, SUBLANES, LANES), F32)],
        compiler_params=pltpu.CompilerParams(vmem_limit_bytes=VMEM_LIMIT),
        name="peer_u",
    )(*idx_parts, x2, gate_t, tbl)


def _peer_v_call(idx_parts, w_t, tbl, h1, mod3, g, seq, tb):
    t = w_t.shape[0]
    per_b = seq // tb
    return pl.pallas_call(
        functools.partial(_peer_v_kernel, tb=tb),
        out_shape=jax.ShapeDtypeStruct((t, D_MODEL), F32),
        grid=(t // tb,),
        in_specs=[pl.BlockSpec((tb // V_SPLIT, PEER_SLOTS), lambda i: (i, 0), memory_space=pltpu.SMEM)] * V_SPLIT
                 + [pl.BlockSpec((tb, PEER_SLOTS), lambda i: (i, 0)),
                    _table_spec(tbl),
                    pl.BlockSpec((tb, D_MODEL), lambda i: (i, 0)),
                    pl.BlockSpec((1, 6, D_MODEL), lambda i: (i // per_b, 0, 0)),
                    pl.BlockSpec((1, D_MODEL), lambda i: (0, 0))],
        out_specs=pl.BlockSpec((tb, D_MODEL), lambda i: (i, 0)),
        scratch_shapes=[pltpu.VMEM((2, PEER_SLOTS * ROW_SUBLANES, 2 * LANES), jnp.uint32),
                        pltpu.VMEM((2, PEER_SLOTS * ROW_SUBLANES, 2 * LANES), jnp.uint32),
                        pltpu.VMEM((tb, 2 * ROW_SUBLANES * PEER_SLOTS), F32),
                        pltpu.VMEM((tb, 2 * ROW_SUBLANES * PEER_SLOTS), F32),
                        pltpu.VMEM((tb, SUBLANES, LANES), F32)],
        compiler_params=pltpu.CompilerParams(vmem_limit_bytes=VMEM_LIMIT),
        name="peer_v",
    )(*idx_parts, w_t, tbl, h1, mod3, g.reshape(1, D_MODEL))


def _pad_lanes(v):
    return jnp.pad(v.reshape(1, -1), ((0, 0), (0, LANES - v.shape[-1])))


def kernel(x, c, ada_w, ada_b, norm1_g, w_in, ssd_conv_w, ssd_conv_b, ssd_dt_bias, ssd_a_log, ssd_d, ssd_norm_g, conf_dw_w, conf_dw_b, conf_ln_g, conf_ln_b, w_out, norm2_g, peer_w_query, peer_sub_keys, peer_u, peer_v, final_norm_g):
    bsz, seq, d = x.shape
    assert d == D_MODEL and ada_w.shape[0] == 1
    t = bsz * seq
    tm = min(512, seq)
    ts = min(256, seq)
    tr = min(256, seq)
    tb = min(512, seq)
    x2 = x.reshape(t, d)

    mod3 = _mod_call(c, ada_w[0], ada_b[0]).reshape(bsz, 6, d)

    wi = w_in[0]
    o1 = SSD_WIDTH
    o2 = o1 + SSD_XBC
    o3 = o2 + SSD_HEADS
    wz = wi[:, :o1].astype(BF16)
    wx = wi[:, o1:o2].astype(BF16)
    wd = jnp.pad(wi[:, o2:o3], ((0, 0), (0, LANES - SSD_HEADS))).astype(BF16)
    wg = wi[:, o3:].astype(BF16)
    z, xbc, glu, dt = _inproj_call(x2, mod3, norm1_g[0], wz, wx, wg, wd, seq, tm)

    h1, hn2 = _mixer_call(
        x2, z, xbc, glu, dt, mod3,
        ssd_conv_w[0], ssd_conv_b[0].reshape(1, -1), _pad_lanes(ssd_dt_bias[0]), _pad_lanes(ssd_a_log[0]),
        jnp.repeat(ssd_d[0], SSD_HEAD_DIM).reshape(1, -1), ssd_norm_g[0].reshape(1, -1),
        conf_dw_w[0], conf_dw_b[0].reshape(1, -1), conf_ln_g[0].reshape(1, -1), conf_ln_b[0].reshape(1, -1),
        w_out[0].astype(BF16), norm2_g[0].reshape(1, -1), bsz, seq, ts)

    keys = peer_sub_keys[0].reshape(PEER_HEADS * 2, PEER_N_KEYS, PEER_D_HALF).astype(BF16)
    idx_t, gate_t = _route_call(hn2, peer_w_query[0].astype(BF16), keys, tr)

    w_t = _peer_u_call(_split_tokens(idx_t, U_SPLIT), hn2, gate_t, _pack_table(peer_u[0]), tb)
    out = _peer_v_call(_split_tokens(idx_t, V_SPLIT), w_t, _pack_table(peer_v[0]), h1, mod3, final_norm_g, seq, tb)
    return out.reshape(bsz, seq, d)
```

```python
import functools
import math

import jax
import jax.numpy as jnp
from jax import lax
from jax.experimental import pallas as pl
from jax.experimental.pallas import tpu as pltpu

F32 = jnp.float32
BF16 = jnp.bfloat16
HIGHEST = lax.Precision.HIGHEST

D_MODEL = 1024
CHUNK = 64
SSD_WIDTH = 512
SSD_HEADS = 8
SSD_HEAD_DIM = 64
SSD_GROUPS = 2
SSD_STATE = 128
SSD_CONV = 4
SSD_XBC = 1024
CONF_WIDTH = 512
CONF_CONV = 31
PEER_HEADS = 8
PEER_N_KEYS = 128
PEER_D_HALF = 128
PEER_TOPK = 16
PEER_SLOTS = PEER_HEADS * PEER_TOPK
NORM_EPS = 1e-6

LANES = 128
SUBLANES = 8
ROW_SUBLANES = D_MODEL // 2 // LANES
VMEM_LIMIT = 56 * 1024 * 1024

V_SPLIT = 8
U_SPLIT = 8

XBC_TAIL = 8
GLU_TAIL = 32


def _silu(v):
    return v * jax.nn.sigmoid(v)


def _softplus(v):
    return jnp.maximum(v, 0.0) + jnp.log(1.0 + jnp.exp(-jnp.abs(v)))


def _bdot(a, b):
    return jnp.dot(a.astype(BF16), b.astype(BF16), preferred_element_type=F32)


def _mod_kernel(c_ref, w_ref, b_ref, o_ref):
    cond = _silu(c_ref[...])
    o_ref[...] = jnp.dot(cond, w_ref[...], precision=HIGHEST, preferred_element_type=F32) + b_ref[...]


def _mod_call(c, ada_w, ada_b):
    bsz, d = c.shape
    n = ada_w.shape[1]
    return pl.pallas_call(
        _mod_kernel,
        out_shape=jax.ShapeDtypeStruct((bsz, n), F32),
        grid=(n // d,),
        in_specs=[pl.BlockSpec((bsz, d), lambda i: (0, 0)),
                  pl.BlockSpec((d, d), lambda i: (0, i)),
                  pl.BlockSpec((1, d), lambda i: (0, i))],
        out_specs=pl.BlockSpec((bsz, d), lambda i: (0, i)),
        name="mod",
    )(c, ada_w, ada_b.reshape(1, n))


def _inproj_kernel(x_ref, mod_ref, g_ref, wz_ref, wx_ref, wg_ref, wd_ref,
                   z_ref, xbc_ref, glu_ref, dt_ref):
    x = x_ref[...]
    ms = jnp.mean(x * x, axis=-1, keepdims=True)
    y = x * lax.rsqrt(ms + NORM_EPS) * g_ref[...]
    sh = mod_ref[0, 0:1, :]
    sc = mod_ref[0, 1:2, :]
    hn = (y * (1.0 + sc) + sh).astype(BF16)
    z_ref[...] = jnp.dot(hn, wz_ref[...], preferred_element_type=F32)
    xbc_ref[...] = jnp.dot(hn, wx_ref[...], preferred_element_type=F32)
    glu_ref[...] = jnp.dot(hn, wg_ref[...], preferred_element_type=F32)
    dt_ref[...] = jnp.dot(hn, wd_ref[...], preferred_element_type=F32)


def _inproj_call(x2, mod3, norm1_g, wz, wx, wg, wd, seq, tm):
    t, d = x2.shape
    per_b = seq // tm
    const = lambda i: (0, 0)
    row = lambda i: (i, 0)
    return pl.pallas_call(
        _inproj_kernel,
        out_shape=(jax.ShapeDtypeStruct((t, SSD_WIDTH), F32),
                   jax.ShapeDtypeStruct((t, SSD_XBC), F32),
                   jax.ShapeDtypeStruct((t, 2 * CONF_WIDTH), F32),
                   jax.ShapeDtypeStruct((t, LANES), F32)),
        grid=(t // tm,),
        in_specs=[pl.BlockSpec((tm, d), row),
                  pl.BlockSpec((1, 6, d), lambda i: (i // per_b, 0, 0)),
                  pl.BlockSpec((1, d), const),
                  pl.BlockSpec(wz.shape, const),
                  pl.BlockSpec(wx.shape, const),
                  pl.BlockSpec(wg.shape, const),
                  pl.BlockSpec(wd.shape, const)],
        out_specs=(pl.BlockSpec((tm, SSD_WIDTH), row),
                   pl.BlockSpec((tm, SSD_XBC), row),
                   pl.BlockSpec((tm, 2 * CONF_WIDTH), row),
                   pl.BlockSpec((tm, LANES), row)),
        compiler_params=pltpu.CompilerParams(vmem_limit_bytes=VMEM_LIMIT),
        name="inproj",
    )(x2, mod3, norm1_g.reshape(1, d), wz, wx, wg, wd)


def _mixer_kernel(x_ref, z_ref, xbc_ref, glu_ref, dt_ref, mod_ref,
                  cw_ref, cb_ref, dtb_ref, alog_ref, dexp_ref, sng_ref,
                  dww_ref, dwb_ref, lng_ref, lnb_ref, wout_ref, n2g_ref,
                  h1_ref, hn2_ref,
                  xext_ref, gext_ref, hst_ref, xc_ref, xdt_ref, acs_ref, eacs_ref,
                  acst_ref, bmt_ref, y_ref, gsh_ref, *, ts):
    nc = ts // CHUNK
    hw = SSD_WIDTH // SSD_GROUPS

    @pl.when(pl.program_id(1) == 0)
    def _():
        xext_ref[0:XBC_TAIL, :] = jnp.zeros((XBC_TAIL, SSD_XBC), F32)
        gext_ref[0:GLU_TAIL, :] = jnp.zeros((GLU_TAIL, CONF_WIDTH), F32)
        hst_ref[...] = jnp.zeros(hst_ref.shape, F32)

    xext_ref[XBC_TAIL:XBC_TAIL + ts, :] = xbc_ref[...]
    acc = cb_ref[...] + cw_ref[0:1, :] * xext_ref[pl.ds(XBC_TAIL - SSD_CONV + 1, ts), :]
    for k in range(1, SSD_CONV):
        acc = acc + cw_ref[k:k + 1, :] * xext_ref[pl.ds(XBC_TAIL - SSD_CONV + 1 + k, ts), :]
    xext_ref[0:XBC_TAIL, :] = xext_ref[ts:ts + XBC_TAIL, :]
    xc_ref[...] = _silu(acc)

    dt = _softplus(dt_ref[...] + dtb_ref[...])
    dta = dt * (-jnp.exp(alog_ref[...]))
    ri = lax.broadcasted_iota(jnp.int32, (ts, ts), 0)
    ci = lax.broadcasted_iota(jnp.int32, (ts, ts), 1)
    ltri = jnp.where(((ri // CHUNK) == (ci // CHUNK)) & (ci <= ri), 1.0, 0.0).astype(F32)
    acs = jnp.dot(ltri, dta, precision=HIGHEST, preferred_element_type=F32)
    acst_ref[...] = acs.T
    er = lax.broadcasted_iota(jnp.int32, (LANES, SSD_WIDTH), 0)
    ec = lax.broadcasted_iota(jnp.int32, (LANES, SSD_WIDTH), 1)
    expand = jnp.where((ec // SSD_HEAD_DIM) == er, 1.0, 0.0).astype(F32)
    dt_exp = jnp.dot(dt, expand, precision=HIGHEST, preferred_element_type=F32)
    acs_exp = jnp.dot(acs, expand, precision=HIGHEST, preferred_element_type=F32)
    acs_ref[...] = acs_exp
    eacs_ref[...] = jnp.exp(acs_exp)
    xdt_ref[...] = xc_ref[:, 0:SSD_WIDTH] * dt_exp
    bmt_ref[...] = xc_ref[:, SSD_WIDTH:SSD_WIDTH + SSD_GROUPS * SSD_STATE].T

    tr = lax.broadcasted_iota(jnp.int32, (CHUNK, CHUNK), 0)
    tc = lax.broadcasted_iota(jnp.int32, (CHUNK, CHUNK), 1)
    tril = tc <= tr
    c_off = SSD_WIDTH + SSD_GROUPS * SSD_STATE

    for c in range(nc):
        r0 = c * CHUNK
        rows = slice(r0, r0 + CHUNK)
        a_last = acs_ref[r0 + CHUNK - 1:r0 + CHUNK, :]
        xw = xdt_ref[rows, :] * jnp.exp(a_last - acs_ref[rows, :])
        cdec = jnp.exp(a_last)
        y_parts = []
        for g in range(SSD_GROUPS):
            cg = xc_ref[rows, c_off + g * SSD_STATE:c_off + (g + 1) * SSD_STATE].astype(BF16)
            bg = xc_ref[rows, SSD_WIDTH + g * SSD_STATE:SSD_WIDTH + (g + 1) * SSD_STATE].astype(BF16)
            cb = lax.dot_general(cg, bg, (((1,), (1,)), ((), ())), preferred_element_type=F32)
            hg = hst_ref[g]
            yoff = jnp.dot(cg, hg.astype(BF16), preferred_element_type=F32)
            st = _bdot(bmt_ref[g * SSD_STATE:(g + 1) * SSD_STATE, rows], xw[:, g * hw:(g + 1) * hw])
            hst_ref[g] = hg * cdec[:, g * hw:(g + 1) * hw] + st
            yds = []
            for hh in range(SSD_HEADS // SSD_GROUPS):
                h = g * (SSD_HEADS // SSD_GROUPS) + hh
                cols = slice(h * SSD_HEAD_DIM, (h + 1) * SSD_HEAD_DIM)
                seg = acs_ref[rows, cols] - acst_ref[h:h + 1, rows]
                dec = jnp.exp(jnp.where(tril, seg, -jnp.inf))
                yds.append(_bdot(cb * dec, xdt_ref[rows, cols]))
            y_parts.append(jnp.concatenate(yds, axis=1) + yoff * eacs_ref[rows, g * hw:(g + 1) * hw])
        y_ref[rows, :] = jnp.concatenate(y_parts, axis=1) + dexp_ref[...] * xc_ref[rows, 0:SSD_WIDTH]

    y = y_ref[...] * _silu(z_ref[...])
    y_ssd = y * lax.rsqrt(jnp.mean(y * y, axis=-1, keepdims=True) + NORM_EPS) * sng_ref[...]

    gext_ref[GLU_TAIL:GLU_TAIL + ts, :] = glu_ref[:, 0:CONF_WIDTH] * jax.nn.sigmoid(glu_ref[:, CONF_WIDTH:])
    span = ts + GLU_TAIL - SUBLANES
    for s in range(1, SUBLANES):
        gsh_ref[s - 1, 0:span, :] = gext_ref[pl.ds(s, span), :]
    u = dwb_ref[...]
    for k in range(CONF_CONV):
        off = GLU_TAIL - CONF_CONV + 1 + k
        s = off % SUBLANES
        rows = pl.ds(off - s, ts)
        u = u + dww_ref[k:k + 1, :] * (gext_ref[rows, :] if s == 0 else gsh_ref[s - 1, rows, :])
    gext_ref[0:GLU_TAIL, :] = gext_ref[ts:ts + GLU_TAIL, :]
    mu = jnp.mean(u, axis=-1, keepdims=True)
    uc = u - mu
    var = jnp.mean(uc * uc, axis=-1, keepdims=True)
    y_conf = _silu(uc * lax.rsqrt(var + NORM_EPS) * lng_ref[...] + lnb_ref[...])

    mix = (jnp.dot(y_ssd.astype(BF16), wout_ref[0:SSD_WIDTH, :], preferred_element_type=F32)
           + jnp.dot(y_conf.astype(BF16), wout_ref[SSD_WIDTH:, :], preferred_element_type=F32))
    h1 = x_ref[...] + mod_ref[0, 2:3, :] * mix
    h1_ref[...] = h1
    hn = h1 * lax.rsqrt(jnp.mean(h1 * h1, axis=-1, keepdims=True) + NORM_EPS) * n2g_ref[...]
    hn2_ref[...] = hn * (1.0 + mod_ref[0, 4:5, :]) + mod_ref[0, 3:4, :]


def _mixer_call(x2, z, xbc, glu, dt, mod3, cw, cb, dtb, alog, dexp, sng, dww, dwb, lng, lnb,
                wout, n2g, bsz, seq, ts):
    t, d = x2.shape
    per_b = seq // ts
    row = lambda b, j: (b * per_b + j, 0)
    const = lambda b, j: (0, 0)

    def full(a):
        return pl.BlockSpec(a.shape, const)

    return pl.pallas_call(
        functools.partial(_mixer_kernel, ts=ts),
        out_shape=(jax.ShapeDtypeStruct((t, d), F32), jax.ShapeDtypeStruct((t, d), F32)),
        grid=(bsz, per_b),
        in_specs=[pl.BlockSpec((ts, d), row),
                  pl.BlockSpec((ts, SSD_WIDTH), row),
                  pl.BlockSpec((ts, SSD_XBC), row),
                  pl.BlockSpec((ts, 2 * CONF_WIDTH), row),
                  pl.BlockSpec((ts, LANES), row),
                  pl.BlockSpec((1, 6, d), lambda b, j: (b, 0, 0)),
                  full(cw), full(cb), full(dtb), full(alog), full(dexp), full(sng),
                  full(dww), full(dwb), full(lng), full(lnb), full(wout), full(n2g)],
        out_specs=(pl.BlockSpec((ts, d), row), pl.BlockSpec((ts, d), row)),
        scratch_shapes=[pltpu.VMEM((ts + XBC_TAIL, SSD_XBC), F32),
                        pltpu.VMEM((ts + GLU_TAIL, CONF_WIDTH), F32),
                        pltpu.VMEM((SSD_GROUPS, SSD_STATE, SSD_WIDTH // SSD_GROUPS), F32),
                        pltpu.VMEM((ts, SSD_XBC), F32),
                        pltpu.VMEM((ts, SSD_WIDTH), F32),
                        pltpu.VMEM((ts, SSD_WIDTH), F32),
                        pltpu.VMEM((ts, SSD_WIDTH), F32),
                        pltpu.VMEM((LANES, ts), F32),
                        pltpu.VMEM((SSD_GROUPS * SSD_STATE, ts), F32),
                        pltpu.VMEM((ts, SSD_WIDTH), F32),
                        pltpu.VMEM((SUBLANES - 1, ts + GLU_TAIL - SUBLANES, CONF_WIDTH), F32)],
        compiler_params=pltpu.CompilerParams(
            dimension_semantics=("arbitrary", "arbitrary"), vmem_limit_bytes=VMEM_LIMIT),
        name="mixer",
    )(x2, z, xbc, glu, dt, mod3, cw, cb, dtb, alog, dexp, sng, dww, dwb, lng, lnb, wout, n2g)


_PAIR_COUNTS = tuple(PEER_TOPK // (a + 1) for a in range(PEER_TOPK))


def _topk_rows(s, val_ref, pick_ref, payload=None):
    n = s.shape[0]
    h = n // 2
    iota = lax.broadcasted_iota(jnp.int32, (h, s.shape[1]), 0).astype(F32)
    a, b = s[:h], s[h:]
    first = a >= b
    hi, lo = jnp.where(first, a, b), jnp.where(first, b, a)
    ihi, ilo = jnp.where(first, iota, iota + float(h)), jnp.where(first, iota + float(h), iota)
    if payload is not None:
        phi, plo = jnp.where(first, payload[:h], payload[h:]), jnp.where(first, payload[h:], payload[:h])
    for r in range(val_ref.shape[0]):
        m = jnp.max(hi, axis=0, keepdims=True)
        am = jnp.min(jnp.where(hi == m, ihi, float(n)), axis=0, keepdims=True)
        val_ref[r:r + 1, :] = m
        hit = ihi == am
        if payload is None:
            pick_ref[r:r + 1, :] = am
        else:
            pick_ref[r:r + 1, :] = jnp.max(jnp.where(hit, phi, -1.0), axis=0, keepdims=True)
            phi = jnp.where(hit, plo, phi)
        hi = jnp.where(hit, lo, hi)
        ihi = jnp.where(hit, ilo, ihi)
        lo = jnp.where(hit, -jnp.inf, lo)


def _route_kernel(hn_ref, wq_ref, keys_ref, idx_ref, gate_ref, q_ref, topv_ref, topi_ref, best_ref, exp_ref):
    q_ref[...] = jnp.dot(hn_ref[...].astype(BF16), wq_ref[...], preferred_element_type=F32)
    nt = (((1,), (1,)), ((), ()))
    n_cand = sum(_PAIR_COUNTS)
    n_pad = -n_cand % (2 * SUBLANES)
    for lt in range(q_ref.shape[0] // LANES):
        toks = slice(lt * LANES, (lt + 1) * LANES)
        for h in range(PEER_HEADS):
            for i in range(2):
                col = (h * 2 + i) * PEER_D_HALF
                qh = q_ref[toks, col:col + PEER_D_HALF].astype(BF16)
                sc = lax.dot_general(keys_ref[h * 2 + i], qh, nt, preferred_element_type=F32)
                _topk_rows(sc, topv_ref.at[i], topi_ref.at[i])
            sv1, sv2 = topv_ref[0], topv_ref[1]
            si1, si2 = topi_ref[0], topi_ref[1]
            cand = jnp.concatenate([sv1[a:a + 1] + sv2[0:nb] for a, nb in enumerate(_PAIR_COUNTS)]
                                   + [jnp.full((n_pad, LANES), -jnp.inf, F32)], axis=0)
            cidx = jnp.concatenate([si1[a:a + 1] * float(PEER_N_KEYS) + si2[0:nb] for a, nb in enumerate(_PAIR_COUNTS)]
                                   + [jnp.zeros((n_pad, LANES), F32)], axis=0)
            _topk_rows(cand, best_ref, exp_ref.at[pl.ds(h * PEER_TOPK, PEER_TOPK)], payload=cidx)
            best = best_ref[...]
            e = jnp.exp(best - best[0:1, :])
            gate_ref[h * PEER_TOPK:(h + 1) * PEER_TOPK, toks] = e / jnp.sum(e, axis=0, keepdims=True)
        idx_ref[toks, :] = (exp_ref[...].T * float(ROW_SUBLANES)).astype(jnp.int32)


def _route_call(hn2, wq, keys, tm):
    t, d = hn2.shape
    return pl.pallas_call(
        _route_kernel,
        out_shape=(jax.ShapeDtypeStruct((t, PEER_SLOTS), jnp.int32),
                   jax.ShapeDtypeStruct((PEER_SLOTS, t), F32)),
        grid=(t // tm,),
        in_specs=[pl.BlockSpec((tm, d), lambda i: (i, 0)),
                  pl.BlockSpec(wq.shape, lambda i: (0, 0)),
                  pl.BlockSpec(keys.shape, lambda i: (0, 0, 0))],
        out_specs=(pl.BlockSpec((tm, PEER_SLOTS), lambda i: (i, 0)),
                   pl.BlockSpec((PEER_SLOTS, tm), lambda i: (0, i))),
        scratch_shapes=[pltpu.VMEM((tm, wq.shape[1]), F32),
                        pltpu.VMEM((2, PEER_TOPK, LANES), F32),
                        pltpu.VMEM((2, PEER_TOPK, LANES), F32),
                        pltpu.VMEM((PEER_TOPK, LANES), F32),
                        pltpu.VMEM((PEER_SLOTS, LANES), F32)],
        compiler_params=pltpu.CompilerParams(vmem_limit_bytes=VMEM_LIMIT),
        name="route",
    )(hn2, wq, keys)


def _pack_kernel(t_ref, o_ref):
    half = t_ref.shape[1] // 2
    lo = pltpu.bitcast(t_ref[:, :half].astype(BF16).astype(F32), jnp.uint32)
    hi = pltpu.bitcast(t_ref[:, half:].astype(BF16).astype(F32), jnp.uint32)
    word = hi | (lo >> 16)
    rows = t_ref.shape[0]
    for s in range(ROW_SUBLANES):
        o_ref[pl.ds(s, rows, stride=ROW_SUBLANES), :] = word[:, s * LANES:(s + 1) * LANES]


def _pack_table(tbl):
    n, d = tbl.shape
    rows = 512
    return pl.pallas_call(
        _pack_kernel,
        out_shape=jax.ShapeDtypeStruct((n * ROW_SUBLANES, LANES), jnp.uint32),
        grid=(n // rows,),
        in_specs=[pl.BlockSpec((rows, d), lambda i: (i, 0))],
        out_specs=pl.BlockSpec((rows * ROW_SUBLANES, LANES), lambda i: (i, 0)),
        name="pack",
    )(tbl)


def _unpack_lo(w):
    return pltpu.bitcast(w << 16, F32)


def _unpack_hi(w):
    return pltpu.bitcast(w & jnp.uint32(0xFFFF0000), F32)


def _split_tokens(idx_t, ways):
    t, n = idx_t.shape
    parts = idx_t.reshape(t // ways, ways, n)
    return [parts[:, k, :] for k in range(ways)]


def _gather_rows(idx_refs, tbl_ref, dsts, p):
    for j in range(PEER_SLOTS):
        for idx_ref, (slot_ref, lane0) in zip(idx_refs, dsts):
            start = pl.multiple_of(idx_ref[p, j], ROW_SUBLANES)
            slot_ref[j * ROW_SUBLANES:(j + 1) * ROW_SUBLANES, lane0:lane0 + LANES] = (
                tbl_ref[pl.ds(start, ROW_SUBLANES), :])


def _peer_u_kernel(*refs, tb):
    idx_refs = refs[:U_SPLIT]
    x_ref, gate_ref, tbl_ref, w_ref, slot_ref, prod_ref, x3_ref = refs[U_SPLIT:]
    per_vreg = SUBLANES // ROW_SUBLANES
    lane = lax.broadcasted_iota(jnp.int32, (PEER_SLOTS, LANES), 1)
    for r in range(SUBLANES):
        x3_ref[:, r, :] = x_ref[:, r * LANES:(r + 1) * LANES]

    def products(slot_ref, prod_ref, t):
        xt = x3_ref[t]
        xlo = jnp.concatenate([xt[0:ROW_SUBLANES]] * per_vreg, axis=0)
        xhi = jnp.concatenate([xt[ROW_SUBLANES:]] * per_vreg, axis=0)
        words = slot_ref[...].reshape(PEER_SLOTS // per_vreg, SUBLANES, LANES)
        prod = _unpack_lo(words) * xlo[None] + _unpack_hi(words) * xhi[None]
        prod_ref[...] = prod.reshape(PEER_SLOTS * ROW_SUBLANES, LANES)

    def reduce_into(prod_ref, tl, acc):
        part = prod_ref[pl.ds(0, PEER_SLOTS, stride=ROW_SUBLANES), :]
        for r in range(1, ROW_SUBLANES):
            part = part + prod_ref[pl.ds(r, PEER_SLOTS, stride=ROW_SUBLANES), :]
        col = jnp.sum(part, axis=-1, keepdims=True)
        return jnp.where(lane == tl, col, acc)

    n = U_SPLIT
    prod_ref[...] = jnp.zeros(prod_ref.shape, F32)
    dsts = [(slot_ref.at[k], 0) for k in range(n)]
    for blk in range(tb // LANES):
        base = blk * LANES
        if blk == 0:
            _gather_rows(idx_refs, tbl_ref, dsts, 0)

        def body(i, acc):
            for k in range(n):
                acc = reduce_into(prod_ref.at[k], n * (i - 1) + k, acc)
            for k in range(n):
                products(slot_ref.at[k], prod_ref.at[k], base + n * i + k)
            _gather_rows(idx_refs, tbl_ref, dsts, jnp.minimum(base // n + i + 1, tb // n - 1))
            return acc

        act = lax.fori_loop(0, LANES // n, body, jnp.zeros((PEER_SLOTS, LANES), F32))
        for k in range(n):
            act = reduce_into(prod_ref.at[k], LANES - n + k, act)
        gelu = 0.5 * act * (1.0 + lax.erf(act * (1.0 / math.sqrt(2.0))))
        w_ref[blk * LANES:(blk + 1) * LANES, :] = (gate_ref[:, blk * LANES:(blk + 1) * LANES] * gelu).T


def _peer_v_kernel(*refs, tb):
    idx_refs = refs[:V_SPLIT]
    w_ref, tbl_ref, h1_ref, mod_ref, g_ref, o_ref, slot_a, slot_b, wrep_hi, wrep_lo, o3_ref = refs[V_SPLIT:]
    cols = 2 * ROW_SUBLANES * PEER_SLOTS
    w = w_ref[...]
    hi = w.astype(BF16)
    lo = (w - hi.astype(F32)).astype(BF16)
    jr = lax.broadcasted_iota(jnp.int32, (PEER_SLOTS, cols), 0)
    jc = lax.broadcasted_iota(jnp.int32, (PEER_SLOTS, cols), 1)
    expand = jnp.where(jc // (2 * ROW_SUBLANES) == jr, 1.0, 0.0).astype(BF16)
    wrep_hi[...] = jnp.dot(hi, expand, preferred_element_type=F32)
    wrep_lo[...] = jnp.dot(lo, expand, preferred_element_type=F32)
    rr = lax.broadcasted_iota(jnp.int32, (SUBLANES, cols), 0)
    rc = lax.broadcasted_iota(jnp.int32, (SUBLANES, cols), 1)
    mask = (rc % (2 * ROW_SUBLANES)) == 2 * (rr % ROW_SUBLANES) + rr // ROW_SUBLANES

    def lhs_rows(t):
        return [jnp.where(mask, jnp.broadcast_to(ref[pl.ds(t, 1), :], (SUBLANES, cols)), 0.0)
                for ref in (wrep_hi, wrep_lo)]

    def store_token(t, val):
        o3_ref[t] = val

    def combine_pair(slot_ref, t):
        lhs = jnp.concatenate(lhs_rows(t) + lhs_rows(t + 1), axis=0).astype(BF16)
        res = jnp.dot(lhs, pltpu.bitcast(slot_ref[...], BF16), preferred_element_type=F32)
        store_token(t, res[0:SUBLANES, 0:LANES] + res[SUBLANES:2 * SUBLANES, 0:LANES])
        store_token(t + 1, res[2 * SUBLANES:3 * SUBLANES, LANES:] + res[3 * SUBLANES:, LANES:])

    pairs = slot_a.shape[0]
    half = 2 * pairs
    per_trip = 2 * half
    assert per_trip == len(idx_refs)
    dsts = [((slot_a, slot_b)[k // half].at[(k // 2) % pairs], (k % 2) * LANES) for k in range(per_trip)]
    _gather_rows(idx_refs, tbl_ref, dsts, 0)

    def body(i, carry):
        t0 = per_trip * i
        for q in range(pairs):
            combine_pair(slot_a.at[q], t0 + 2 * q)
        for q in range(pairs):
            combine_pair(slot_b.at[q], t0 + half + 2 * q)
        _gather_rows(idx_refs, tbl_ref, dsts, jnp.minimum(i + 1, tb // per_trip - 1))
        return carry

    lax.fori_loop(0, tb // per_trip, body, 0)
    for r in range(SUBLANES):
        cols = slice(r * LANES, (r + 1) * LANES)
        o_ref[:, cols] = h1_ref[:, cols] + mod_ref[0, 5:6, cols] * o3_ref[:, r, :]
    h = o_ref[...]
    o_ref[...] = h * lax.rsqrt(jnp.mean(h * h, axis=-1, keepdims=True) + NORM_EPS) * g_ref[...]


def _table_spec(tbl):
    return pl.BlockSpec(tbl.shape, lambda i: (0, 0), pipeline_mode=pl.Buffered(1))


def _peer_u_call(idx_parts, x2, gate_t, tbl, tb):
    t, d = x2.shape
    return pl.pallas_call(
        functools.partial(_peer_u_kernel, tb=tb),
        out_shape=jax.ShapeDtypeStruct((t, PEER_SLOTS), F32),
        grid=(t // tb,),
        in_specs=[pl.BlockSpec((tb // U_SPLIT, PEER_SLOTS), lambda i: (i, 0), memory_space=pltpu.SMEM)] * U_SPLIT
                 + [pl.BlockSpec((tb, d), lambda i: (i, 0)),
                    pl.BlockSpec((PEER_SLOTS, tb), lambda i: (0, i)),
                    _table_spec(tbl)],
        out_specs=pl.BlockSpec((tb, PEER_SLOTS), lambda i: (i, 0)),
        scratch_shapes=[pltpu.VMEM((U_SPLIT, PEER_SLOTS * ROW_SUBLANES, LANES), jnp.uint32),
                        pltpu.VMEM((U_SPLIT, PEER_SLOTS * ROW_SUBLANES, LANES), F32),
                        pltpu.VMEM((tb, SUBLANES, LANES), F32)],
        compiler_params=pltpu.CompilerParams(vmem_limit_bytes=VMEM_LIMIT),
        name="peer_u",
    )(*idx_parts, x2, gate_t, tbl)


def _peer_v_call(idx_parts, w_t, tbl, h1, mod3, g, seq, tb):
    t = w_t.shape[0]
    per_b = seq // tb
    return pl.pallas_call(
        functools.partial(_peer_v_kernel, tb=tb),
        out_shape=jax.ShapeDtypeStruct((t, D_MODEL), F32),
        grid=(t // tb,),
        in_specs=[pl.BlockSpec((tb // V_SPLIT, PEER_SLOTS), lambda i: (i, 0), memory_space=pltpu.SMEM)] * V_SPLIT
                 + [pl.BlockSpec((tb, PEER_SLOTS), lambda i: (i, 0)),
                    _table_spec(tbl),
                    pl.BlockSpec((tb, D_MODEL), lambda i: (i, 0)),
                    pl.BlockSpec((1, 6, D_MODEL), lambda i: (i // per_b, 0, 0)),
                    pl.BlockSpec((1, D_MODEL), lambda i: (0, 0))],
        out_specs=pl.BlockSpec((tb, D_MODEL), lambda i: (i, 0)),
        scratch_shapes=[pltpu.VMEM((2, PEER_SLOTS * ROW_SUBLANES, 2 * LANES), jnp.uint32),
                        pltpu.VMEM((2, PEER_SLOTS * ROW_SUBLANES, 2 * LANES), jnp.uint32),
                        pltpu.VMEM((tb, 2 * ROW_SUBLANES * PEER_SLOTS), F32),
                        pltpu.VMEM((tb, 2 * ROW_SUBLANES * PEER_SLOTS), F32),
                        pltpu.VMEM((tb, SUBLANES, LANES), F32)],
        compiler_params=pltpu.CompilerParams(vmem_limit_bytes=VMEM_LIMIT),
        name="peer_v",
    )(*idx_parts, w_t, tbl, h1, mod3, g.reshape(1, D_MODEL))


def _pad_lanes(v):
    return jnp.pad(v.reshape(1, -1), ((0, 0), (0, LANES - v.shape[-1])))


def kernel(x, c, ada_w, ada_b, norm1_g, w_in, ssd_conv_w, ssd_conv_b, ssd_dt_bias, ssd_a_log, ssd_d, ssd_norm_g, conf_dw_w, conf_dw_b, conf_ln_g, conf_ln_b, w_out, norm2_g, peer_w_query, peer_sub_keys, peer_u, peer_v, final_norm_g):
    bsz, seq, d = x.shape
    assert d == D_MODEL and ada_w.shape[0] == 1
    t = bsz * seq
    tm = min(512, seq)
    ts = min(256, seq)
    tr = min(256, seq)
    tb = min(512, seq)
    x2 = x.reshape(t, d)

    mod3 = _mod_call(c, ada_w[0], ada_b[0]).reshape(bsz, 6, d)

    wi = w_in[0]
    o1 = SSD_WIDTH
    o2 = o1 + SSD_XBC
    o3 = o2 + SSD_HEADS
    wz = wi[:, :o1].astype(BF16)
    wx = wi[:, o1:o2].astype(BF16)
    wd = jnp.pad(wi[:, o2:o3], ((0, 0), (0, LANES - SSD_HEADS))).astype(BF16)
    wg = wi[:, o3:].astype(BF16)
    z, xbc, glu, dt = _inproj_call(x2, mod3, norm1_g[0], wz, wx, wg, wd, seq, tm)

    h1, hn2 = _mixer_call(
        x2, z, xbc, glu, dt, mod3,
        ssd_conv_w[0], ssd_conv_b[0].reshape(1, -1), _pad_lanes(ssd_dt_bias[0]), _pad_lanes(ssd_a_log[0]),
        jnp.repeat(ssd_d[0], SSD_HEAD_DIM).reshape(1, -1), ssd_norm_g[0].reshape(1, -1),
        conf_dw_w[0], conf_dw_b[0].reshape(1, -1), conf_ln_g[0].reshape(1, -1), conf_ln_b[0].reshape(1, -1),
        w_out[0].astype(BF16), norm2_g[0].reshape(1, -1), bsz, seq, ts)

    keys = peer_sub_keys[0].reshape(PEER_HEADS * 2, PEER_N_KEYS, PEER_D_HALF).astype(BF16)
    idx_t, gate_t = _route_call(hn2, peer_w_query[0].astype(BF16), keys, tr)

    w_t = _peer_u_call(_split_tokens(idx_t, U_SPLIT), hn2, gate_t, _pack_table(peer_u[0]), tb)
    out = _peer_v_call(_split_tokens(idx_t, V_SPLIT), w_t, _pack_table(peer_v[0]), h1, mod3, final_norm_g, seq, tb)
    return out.reshape(bsz, seq, d)
```

```python
import functools
import math

import jax
import jax.numpy as jnp
from jax import lax
from jax.experimental import pallas as pl
from jax.experimental.pallas import tpu as pltpu

F32 = jnp.float32
BF16 = jnp.bfloat16
HIGHEST = lax.Precision.HIGHEST

D_MODEL = 1024
CHUNK = 64
SSD_WIDTH = 512
SSD_HEADS = 8
SSD_HEAD_DIM = 64
SSD_GROUPS = 2
SSD_STATE = 128
SSD_CONV = 4
SSD_XBC = 1024
CONF_WIDTH = 512
CONF_CONV = 31
PEER_HEADS = 8
PEER_N_KEYS = 128
PEER_D_HALF = 128
PEER_TOPK = 16
PEER_SLOTS = PEER_HEADS * PEER_TOPK
NORM_EPS = 1e-6

LANES = 128
SUBLANES = 8
ROW_SUBLANES = D_MODEL // 2 // LANES
VMEM_LIMIT = 56 * 1024 * 1024

V_SPLIT = 8
U_SPLIT = V_SPLIT

XBC_TAIL = 8
GLU_TAIL = 32


def _silu(v):
    return v * jax.nn.sigmoid(v)


def _softplus(v):
    return jnp.maximum(v, 0.0) + jnp.log(1.0 + jnp.exp(-jnp.abs(v)))


def _bdot(a, b):
    return jnp.dot(a.astype(BF16), b.astype(BF16), preferred_element_type=F32)


def _mod_kernel(c_ref, w_ref, b_ref, o_ref):
    cond = _silu(c_ref[...])
    o_ref[...] = jnp.dot(cond, w_ref[...], precision=HIGHEST, preferred_element_type=F32) + b_ref[...]


def _mod_call(c, ada_w, ada_b):
    bsz, d = c.shape
    n = ada_w.shape[1]
    return pl.pallas_call(
        _mod_kernel,
        out_shape=jax.ShapeDtypeStruct((bsz, n), F32),
        grid=(n // d,),
        in_specs=[pl.BlockSpec((bsz, d), lambda i: (0, 0)),
                  pl.BlockSpec((d, d), lambda i: (0, i)),
                  pl.BlockSpec((1, d), lambda i: (0, i))],
        out_specs=pl.BlockSpec((bsz, d), lambda i: (0, i)),
        name="mod",
    )(c, ada_w, ada_b.reshape(1, n))


def _inproj_kernel(x_ref, mod_ref, g_ref, wz_ref, wx_ref, wg_ref, wd_ref,
                   z_ref, xbc_ref, glu_ref, dt_ref):
    x = x_ref[...]
    ms = jnp.mean(x * x, axis=-1, keepdims=True)
    y = x * lax.rsqrt(ms + NORM_EPS) * g_ref[...]
    sh = mod_ref[0, 0:1, :]
    sc = mod_ref[0, 1:2, :]
    hn = (y * (1.0 + sc) + sh).astype(BF16)
    z_ref[...] = jnp.dot(hn, wz_ref[...], preferred_element_type=F32)
    xbc_ref[...] = jnp.dot(hn, wx_ref[...], preferred_element_type=F32)
    glu_ref[...] = jnp.dot(hn, wg_ref[...], preferred_element_type=F32)
    dt_ref[...] = jnp.dot(hn, wd_ref[...], preferred_element_type=F32)


def _inproj_call(x2, mod3, norm1_g, wz, wx, wg, wd, seq, tm):
    t, d = x2.shape
    per_b = seq // tm
    const = lambda i: (0, 0)
    row = lambda i: (i, 0)
    return pl.pallas_call(
        _inproj_kernel,
        out_shape=(jax.ShapeDtypeStruct((t, SSD_WIDTH), F32),
                   jax.ShapeDtypeStruct((t, SSD_XBC), F32),
                   jax.ShapeDtypeStruct((t, 2 * CONF_WIDTH), F32),
                   jax.ShapeDtypeStruct((t, LANES), F32)),
        grid=(t // tm,),
        in_specs=[pl.BlockSpec((tm, d), row),
                  pl.BlockSpec((1, 6, d), lambda i: (i // per_b, 0, 0)),
                  pl.BlockSpec((1, d), const),
                  pl.BlockSpec(wz.shape, const),
                  pl.BlockSpec(wx.shape, const),
                  pl.BlockSpec(wg.shape, const),
                  pl.BlockSpec(wd.shape, const)],
        out_specs=(pl.BlockSpec((tm, SSD_WIDTH), row),
                   pl.BlockSpec((tm, SSD_XBC), row),
                   pl.BlockSpec((tm, 2 * CONF_WIDTH), row),
                   pl.BlockSpec((tm, LANES), row)),
        compiler_params=pltpu.CompilerParams(vmem_limit_bytes=VMEM_LIMIT),
        name="inproj",
    )(x2, mod3, norm1_g.reshape(1, d), wz, wx, wg, wd)


def _mixer_kernel(x_ref, z_ref, xbc_ref, glu_ref, dt_ref, mod_ref,
                  cw_ref, cb_ref, dtb_ref, alog_ref, dexp_ref, sng_ref,
                  dww_ref, dwb_ref, lng_ref, lnb_ref, wout_ref, n2g_ref,
                  h1_ref, hn2_ref,
                  xext_ref, gext_ref, hst_ref, xc_ref, xdt_ref, acs_ref, eacs_ref,
                  acst_ref, bmt_ref, y_ref, gsh_ref, *, ts):
    nc = ts // CHUNK
    hw = SSD_WIDTH // SSD_GROUPS

    @pl.when(pl.program_id(1) == 0)
    def _():
        xext_ref[0:XBC_TAIL, :] = jnp.zeros((XBC_TAIL, SSD_XBC), F32)
        gext_ref[0:GLU_TAIL, :] = jnp.zeros((GLU_TAIL, CONF_WIDTH), F32)
        hst_ref[...] = jnp.zeros(hst_ref.shape, F32)

    xext_ref[XBC_TAIL:XBC_TAIL + ts, :] = xbc_ref[...]
    acc = cb_ref[...] + cw_ref[0:1, :] * xext_ref[pl.ds(XBC_TAIL - SSD_CONV + 1, ts), :]
    for k in range(1, SSD_CONV):
        acc = acc + cw_ref[k:k + 1, :] * xext_ref[pl.ds(XBC_TAIL - SSD_CONV + 1 + k, ts), :]
    xext_ref[0:XBC_TAIL, :] = xext_ref[ts:ts + XBC_TAIL, :]
    xc_ref[...] = _silu(acc)

    dt = _softplus(dt_ref[...] + dtb_ref[...])
    dta = dt * (-jnp.exp(alog_ref[...]))
    ri = lax.broadcasted_iota(jnp.int32, (ts, ts), 0)
    ci = lax.broadcasted_iota(jnp.int32, (ts, ts), 1)
    ltri = jnp.where(((ri // CHUNK) == (ci // CHUNK)) & (ci <= ri), 1.0, 0.0).astype(F32)
    acs = jnp.dot(ltri, dta, precision=HIGHEST, preferred_element_type=F32)
    acst_ref[...] = acs.T
    er = lax.broadcasted_iota(jnp.int32, (LANES, SSD_WIDTH), 0)
    ec = lax.broadcasted_iota(jnp.int32, (LANES, SSD_WIDTH), 1)
    expand = jnp.where((ec // SSD_HEAD_DIM) == er, 1.0, 0.0).astype(F32)
    dt_exp = jnp.dot(dt, expand, precision=HIGHEST, preferred_element_type=F32)
    acs_exp = jnp.dot(acs, expand, precision=HIGHEST, preferred_element_type=F32)
    acs_ref[...] = acs_exp
    eacs_ref[...] = jnp.exp(acs_exp)
    xdt_ref[...] = xc_ref[:, 0:SSD_WIDTH] * dt_exp
    bmt_ref[...] = xc_ref[:, SSD_WIDTH:SSD_WIDTH + SSD_GROUPS * SSD_STATE].T

    tr = lax.broadcasted_iota(jnp.int32, (CHUNK, CHUNK), 0)
    tc = lax.broadcasted_iota(jnp.int32, (CHUNK, CHUNK), 1)
    tril = tc <= tr
    c_off = SSD_WIDTH + SSD_GROUPS * SSD_STATE

    for c in range(nc):
        r0 = c * CHUNK
        rows = slice(r0, r0 + CHUNK)
        a_last = acs_ref[r0 + CHUNK - 1:r0 + CHUNK, :]
        xw = xdt_ref[rows, :] * jnp.exp(a_last - acs_ref[rows, :])
        cdec = jnp.exp(a_last)
        y_parts = []
        for g in range(SSD_GROUPS):
            cg = xc_ref[rows, c_off + g * SSD_STATE:c_off + (g + 1) * SSD_STATE].astype(BF16)
            bg = xc_ref[rows, SSD_WIDTH + g * SSD_STATE:SSD_WIDTH + (g + 1) * SSD_STATE].astype(BF16)
            cb = lax.dot_general(cg, bg, (((1,), (1,)), ((), ())), preferred_element_type=F32)
            hg = hst_ref[g]
            yoff = jnp.dot(cg, hg.astype(BF16), preferred_element_type=F32)
            st = _bdot(bmt_ref[g * SSD_STATE:(g + 1) * SSD_STATE, rows], xw[:, g * hw:(g + 1) * hw])
            hst_ref[g] = hg * cdec[:, g * hw:(g + 1) * hw] + st
            yds = []
            for hh in range(SSD_HEADS // SSD_GROUPS):
                h = g * (SSD_HEADS // SSD_GROUPS) + hh
                cols = slice(h * SSD_HEAD_DIM, (h + 1) * SSD_HEAD_DIM)
                seg = acs_ref[rows, cols] - acst_ref[h:h + 1, rows]
                dec = jnp.exp(jnp.where(tril, seg, -jnp.inf))
                yds.append(_bdot(cb * dec, xdt_ref[rows, cols]))
            y_parts.append(jnp.concatenate(yds, axis=1) + yoff * eacs_ref[rows, g * hw:(g + 1) * hw])
        y_ref[rows, :] = jnp.concatenate(y_parts, axis=1) + dexp_ref[...] * xc_ref[rows, 0:SSD_WIDTH]

    y = y_ref[...] * _silu(z_ref[...])
    y_ssd = y * lax.rsqrt(jnp.mean(y * y, axis=-1, keepdims=True) + NORM_EPS) * sng_ref[...]

    gext_ref[GLU_TAIL:GLU_TAIL + ts, :] = glu_ref[:, 0:CONF_WIDTH] * jax.nn.sigmoid(glu_ref[:, CONF_WIDTH:])
    span = ts + GLU_TAIL - SUBLANES
    for s in range(1, SUBLANES):
        gsh_ref[s - 1, 0:span, :] = gext_ref[pl.ds(s, span), :]
    u = dwb_ref[...]
    for k in range(CONF_CONV):
        off = GLU_TAIL - CONF_CONV + 1 + k
        s = off % SUBLANES
        rows = pl.ds(off - s, ts)
        u = u + dww_ref[k:k + 1, :] * (gext_ref[rows, :] if s == 0 else gsh_ref[s - 1, rows, :])
    gext_ref[0:GLU_TAIL, :] = gext_ref[ts:ts + GLU_TAIL, :]
    mu = jnp.mean(u, axis=-1, keepdims=True)
    uc = u - mu
    var = jnp.mean(uc * uc, axis=-1, keepdims=True)
    y_conf = _silu(uc * lax.rsqrt(var + NORM_EPS) * lng_ref[...] + lnb_ref[...])

    mix = (jnp.dot(y_ssd.astype(BF16), wout_ref[0:SSD_WIDTH, :], preferred_element_type=F32)
           + jnp.dot(y_conf.astype(BF16), wout_ref[SSD_WIDTH:, :], preferred_element_type=F32))
    h1 = x_ref[...] + mod_ref[0, 2:3, :] * mix
    h1_ref[...] = h1
    hn = h1 * lax.rsqrt(jnp.mean(h1 * h1, axis=-1, keepdims=True) + NORM_EPS) * n2g_ref[...]
    hn2_ref[...] = hn * (1.0 + mod_ref[0, 4:5, :]) + mod_ref[0, 3:4, :]


def _mixer_call(x2, z, xbc, glu, dt, mod3, cw, cb, dtb, alog, dexp, sng, dww, dwb, lng, lnb,
                wout, n2g, bsz, seq, ts):
    t, d = x2.shape
    per_b = seq // ts
    row = lambda b, j: (b * per_b + j, 0)
    const = lambda b, j: (0, 0)

    def full(a):
        return pl.BlockSpec(a.shape, const)

    return pl.pallas_call(
        functools.partial(_mixer_kernel, ts=ts),
        out_shape=(jax.ShapeDtypeStruct((t, d), F32), jax.ShapeDtypeStruct((t, d), F32)),
        grid=(bsz, per_b),
        in_specs=[pl.BlockSpec((ts, d), row),
                  pl.BlockSpec((ts, SSD_WIDTH), row),
                  pl.BlockSpec((ts, SSD_XBC), row),
                  pl.BlockSpec((ts, 2 * CONF_WIDTH), row),
                  pl.BlockSpec((ts, LANES), row),
                  pl.BlockSpec((1, 6, d), lambda b, j: (b, 0, 0)),
                  full(cw), full(cb), full(dtb), full(alog), full(dexp), full(sng),
                  full(dww), full(dwb), full(lng), full(lnb), full(wout), full(n2g)],
        out_specs=(pl.BlockSpec((ts, d), row), pl.BlockSpec((ts, d), row)),
        scratch_shapes=[pltpu.VMEM((ts + XBC_TAIL, SSD_XBC), F32),
                        pltpu.VMEM((ts + GLU_TAIL, CONF_WIDTH), F32),
                        pltpu.VMEM((SSD_GROUPS, SSD_STATE, SSD_WIDTH // SSD_GROUPS), F32),
                        pltpu.VMEM((ts, SSD_XBC), F32),
                        pltpu.VMEM((ts, SSD_WIDTH), F32),
                        pltpu.VMEM((ts, SSD_WIDTH), F32),
                        pltpu.VMEM((ts, SSD_WIDTH), F32),
                        pltpu.VMEM((LANES, ts), F32),
                        pltpu.VMEM((SSD_GROUPS * SSD_STATE, ts), F32),
                        pltpu.VMEM((ts, SSD_WIDTH), F32),
                        pltpu.VMEM((SUBLANES - 1, ts + GLU_TAIL - SUBLANES, CONF_WIDTH), F32)],
        compiler_params=pltpu.CompilerParams(
            dimension_semantics=("arbitrary", "arbitrary"), vmem_limit_bytes=VMEM_LIMIT),
        name="mixer",
    )(x2, z, xbc, glu, dt, mod3, cw, cb, dtb, alog, dexp, sng, dww, dwb, lng, lnb, wout, n2g)


_PAIR_COUNTS = tuple(PEER_TOPK // (a + 1) for a in range(PEER_TOPK))


def _topk_rows(s, val_ref, pick_ref, payload=None):
    n = s.shape[0]
    h = n // 2
    iota = lax.broadcasted_iota(jnp.int32, (h, s.shape[1]), 0).astype(F32)
    a, b = s[:h], s[h:]
    first = a >= b
    hi, lo = jnp.where(first, a, b), jnp.where(first, b, a)
    ihi, ilo = jnp.where(first, iota, iota + float(h)), jnp.where(first, iota + float(h), iota)
    if payload is not None:
        phi, plo = jnp.where(first, payload[:h], payload[h:]), jnp.where(first, payload[h:], payload[:h])
    for r in range(val_ref.shape[0]):
        m = jnp.max(hi, axis=0, keepdims=True)
        am = jnp.min(jnp.where(hi == m, ihi, float(n)), axis=0, keepdims=True)
        val_ref[r:r + 1, :] = m
        hit = ihi == am
        if payload is None:
            pick_ref[r:r + 1, :] = am
        else:
            pick_ref[r:r + 1, :] = jnp.max(jnp.where(hit, phi, -1.0), axis=0, keepdims=True)
            phi = jnp.where(hit, plo, phi)
        hi = jnp.where(hit, lo, hi)
        ihi = jnp.where(hit, ilo, ihi)
        lo = jnp.where(hit, -jnp.inf, lo)


def _route_kernel(hn_ref, wq_ref, keys_ref, *refs):
    idx_refs = refs[:V_SPLIT]
    gate_ref, q_ref, topv_ref, topi_ref, best_ref, exp_ref, idx_scr = refs[V_SPLIT:]
    q_ref[...] = jnp.dot(hn_ref[...].astype(BF16), wq_ref[...], preferred_element_type=F32)
    nt = (((1,), (1,)), ((), ()))
    n_cand = sum(_PAIR_COUNTS)
    n_pad = -n_cand % (2 * SUBLANES)
    for lt in range(q_ref.shape[0] // LANES):
        toks = slice(lt * LANES, (lt + 1) * LANES)
        for h in range(PEER_HEADS):
            for i in range(2):
                col = (h * 2 + i) * PEER_D_HALF
                qh = q_ref[toks, col:col + PEER_D_HALF].astype(BF16)
                sc = lax.dot_general(keys_ref[h * 2 + i], qh, nt, preferred_element_type=F32)
                _topk_rows(sc, topv_ref.at[i], topi_ref.at[i])
            sv1, sv2 = topv_ref[0], topv_ref[1]
            si1, si2 = topi_ref[0], topi_ref[1]
            cand = jnp.concatenate([sv1[a:a + 1] + sv2[0:nb] for a, nb in enumerate(_PAIR_COUNTS)]
                                   + [jnp.full((n_pad, LANES), -jnp.inf, F32)], axis=0)
            cidx = jnp.concatenate([si1[a:a + 1] * float(PEER_N_KEYS) + si2[0:nb] for a, nb in enumerate(_PAIR_COUNTS)]
                                   + [jnp.zeros((n_pad, LANES), F32)], axis=0)
            _topk_rows(cand, best_ref, exp_ref.at[pl.ds(h * PEER_TOPK, PEER_TOPK)], payload=cidx)
            best = best_ref[...]
            e = jnp.exp(best - best[0:1, :])
            gate_ref[h * PEER_TOPK:(h + 1) * PEER_TOPK, toks] = e / jnp.sum(e, axis=0, keepdims=True)
        idx_scr[...] = (exp_ref[...].T * float(ROW_SUBLANES)).astype(jnp.int32)
        per = LANES // V_SPLIT
        for k in range(V_SPLIT):
            idx_refs[k][lt * per:(lt + 1) * per, :] = idx_scr[pl.ds(k, per, stride=V_SPLIT), :]


def _route_call(hn2, wq, keys, tm):
    t, d = hn2.shape
    return pl.pallas_call(
        _route_kernel,
        out_shape=[jax.ShapeDtypeStruct((t // V_SPLIT, PEER_SLOTS), jnp.int32)] * V_SPLIT
                  + [jax.ShapeDtypeStruct((PEER_SLOTS, t), F32)],
        grid=(t // tm,),
        in_specs=[pl.BlockSpec((tm, d), lambda i: (i, 0)),
                  pl.BlockSpec(wq.shape, lambda i: (0, 0)),
                  pl.BlockSpec(keys.shape, lambda i: (0, 0, 0))],
        out_specs=[pl.BlockSpec((tm // V_SPLIT, PEER_SLOTS), lambda i: (i, 0))] * V_SPLIT
                  + [pl.BlockSpec((PEER_SLOTS, tm), lambda i: (0, i))],
        scratch_shapes=[pltpu.VMEM((tm, wq.shape[1]), F32),
                        pltpu.VMEM((2, PEER_TOPK, LANES), F32),
                        pltpu.VMEM((2, PEER_TOPK, LANES), F32),
                        pltpu.VMEM((PEER_TOPK, LANES), F32),
                        pltpu.VMEM((PEER_SLOTS, LANES), F32),
                        pltpu.VMEM((LANES, PEER_SLOTS), jnp.int32)],
        compiler_params=pltpu.CompilerParams(vmem_limit_bytes=VMEM_LIMIT),
        name="route",
    )(hn2, wq, keys)


def _pack_kernel(t_ref, o_ref):
    half = t_ref.shape[1] // 2
    lo = pltpu.bitcast(t_ref[:, :half].astype(BF16).astype(F32), jnp.uint32)
    hi = pltpu.bitcast(t_ref[:, half:].astype(BF16).astype(F32), jnp.uint32)
    word = hi | (lo >> 16)
    rows = t_ref.shape[0]
    for s in range(ROW_SUBLANES):
        o_ref[pl.ds(s, rows, stride=ROW_SUBLANES), :] = word[:, s * LANES:(s + 1) * LANES]


def _pack_table(tbl):
    n, d = tbl.shape
    rows = 512
    return pl.pallas_call(
        _pack_kernel,
        out_shape=jax.ShapeDtypeStruct((n * ROW_SUBLANES, LANES), jnp.uint32),
        grid=(n // rows,),
        in_specs=[pl.BlockSpec((rows, d), lambda i: (i, 0))],
        out_specs=pl.BlockSpec((rows * ROW_SUBLANES, LANES), lambda i: (i, 0)),
        name="pack",
    )(tbl)


def _unpack_lo(w):
    return pltpu.bitcast(w << 16, F32)


def _unpack_hi(w):
    return pltpu.bitcast(w & jnp.uint32(0xFFFF0000), F32)


def _gather_rows(idx_refs, tbl_ref, dsts, p):
    for j in range(PEER_SLOTS):
        for idx_ref, (slot_ref, lane0) in zip(idx_refs, dsts):
            start = pl.multiple_of(idx_ref[p, j], ROW_SUBLANES)
            slot_ref[j * ROW_SUBLANES:(j + 1) * ROW_SUBLANES, lane0:lane0 + LANES] = (
                tbl_ref[pl.ds(start, ROW_SUBLANES), :])


def _peer_u_kernel(*refs, tb):
    idx_refs = refs[:U_SPLIT]
    x_ref, gate_ref, tbl_ref, w_ref, slot_ref, prod_ref, x3_ref = refs[U_SPLIT:]
    per_vreg = SUBLANES // ROW_SUBLANES
    lane = lax.broadcasted_iota(jnp.int32, (PEER_SLOTS, LANES), 1)
    for r in range(SUBLANES):
        x3_ref[:, r, :] = x_ref[:, r * LANES:(r + 1) * LANES]

    def products(slot_ref, prod_ref, t):
        xt = x3_ref[t]
        xlo = jnp.concatenate([xt[0:ROW_SUBLANES]] * per_vreg, axis=0)
        xhi = jnp.concatenate([xt[ROW_SUBLANES:]] * per_vreg, axis=0)
        words = slot_ref[...].reshape(PEER_SLOTS // per_vreg, SUBLANES, LANES)
        prod = _unpack_lo(words) * xlo[None] + _unpack_hi(words) * xhi[None]
        prod_ref[...] = prod.reshape(PEER_SLOTS * ROW_SUBLANES, LANES)

    def reduce_into(prod_ref, tl, acc):
        part = prod_ref[pl.ds(0, PEER_SLOTS, stride=ROW_SUBLANES), :]
        for r in range(1, ROW_SUBLANES):
            part = part + prod_ref[pl.ds(r, PEER_SLOTS, stride=ROW_SUBLANES), :]
        col = jnp.sum(part, axis=-1, keepdims=True)
        return jnp.where(lane == tl, col, acc)

    n = U_SPLIT
    prod_ref[...] = jnp.zeros(prod_ref.shape, F32)
    dsts = [(slot_ref.at[k], 0) for k in range(n)]
    for blk in range(tb // LANES):
        base = blk * LANES
        if blk == 0:
            _gather_rows(idx_refs, tbl_ref, dsts, 0)

        def body(i, acc):
            for k in range(n):
                acc = reduce_into(prod_ref.at[k], n * (i - 1) + k, acc)
            for k in range(n):
                products(slot_ref.at[k], prod_ref.at[k], base + n * i + k)
            _gather_rows(idx_refs, tbl_ref, dsts, jnp.minimum(base // n + i + 1, tb // n - 1))
            return acc

        act = lax.fori_loop(0, LANES // n, body, jnp.zeros((PEER_SLOTS, LANES), F32))
        for k in range(n):
            act = reduce_into(prod_ref.at[k], LANES - n + k, act)
        gelu = 0.5 * act * (1.0 + lax.erf(act * (1.0 / math.sqrt(2.0))))
        w_ref[blk * LANES:(blk + 1) * LANES, :] = (gate_ref[:, blk * LANES:(blk + 1) * LANES] * gelu).T


def _peer_v_kernel(*refs, tb):
    idx_refs = refs[:V_SPLIT]
    w_ref, tbl_ref, h1_ref, mod_ref, g_ref, o_ref, slot_a, slot_b, wrep_hi, wrep_lo, o3_ref = refs[V_SPLIT:]
    cols = 2 * ROW_SUBLANES * PEER_SLOTS
    w = w_ref[...]
    hi = w.astype(BF16)
    lo = (w - hi.astype(F32)).astype(BF16)
    jr = lax.broadcasted_iota(jnp.int32, (PEER_SLOTS, cols), 0)
    jc = lax.broadcasted_iota(jnp.int32, (PEER_SLOTS, cols), 1)
    expand = jnp.where(jc // (2 * ROW_SUBLANES) == jr, 1.0, 0.0).astype(BF16)
    wrep_hi[...] = jnp.dot(hi, expand, preferred_element_type=F32)
    wrep_lo[...] = jnp.dot(lo, expand, preferred_element_type=F32)
    rr = lax.broadcasted_iota(jnp.int32, (SUBLANES, cols), 0)
    rc = lax.broadcasted_iota(jnp.int32, (SUBLANES, cols), 1)
    mask = (rc % (2 * ROW_SUBLANES)) == 2 * (rr % ROW_SUBLANES) + rr // ROW_SUBLANES

    def lhs_rows(t):
        return [jnp.where(mask, jnp.broadcast_to(ref[pl.ds(t, 1), :], (SUBLANES, cols)), 0.0)
                for ref in (wrep_hi, wrep_lo)]

    def store_token(t, val):
        o3_ref[t] = val

    def combine_pair(slot_ref, t):
        lhs = jnp.concatenate(lhs_rows(t) + lhs_rows(t + 1), axis=0).astype(BF16)
        res = jnp.dot(lhs, pltpu.bitcast(slot_ref[...], BF16), preferred_element_type=F32)
        store_token(t, res[0:SUBLANES, 0:LANES] + res[SUBLANES:2 * SUBLANES, 0:LANES])
        store_token(t + 1, res[2 * SUBLANES:3 * SUBLANES, LANES:] + res[3 * SUBLANES:, LANES:])

    pairs = slot_a.shape[0]
    half = 2 * pairs
    per_trip = 2 * half
    assert per_trip == len(idx_refs)
    dsts = [((slot_a, slot_b)[k // half].at[(k // 2) % pairs], (k % 2) * LANES) for k in range(per_trip)]
    _gather_rows(idx_refs, tbl_ref, dsts, 0)

    def body(i, carry):
        t0 = per_trip * i
        for q in range(pairs):
            combine_pair(slot_a.at[q], t0 + 2 * q)
        for q in range(pairs):
            combine_pair(slot_b.at[q], t0 + half + 2 * q)
        _gather_rows(idx_refs, tbl_ref, dsts, jnp.minimum(i + 1, tb // per_trip - 1))
        return carry

    lax.fori_loop(0, tb // per_trip, body, 0)
    for r in range(SUBLANES):
        cols = slice(r * LANES, (r + 1) * LANES)
        o_ref[:, cols] = h1_ref[:, cols] + mod_ref[0, 5:6, cols] * o3_ref[:, r, :]
    h = o_ref[...]
    o_ref[...] = h * lax.rsqrt(jnp.mean(h * h, axis=-1, keepdims=True) + NORM_EPS) * g_ref[...]


def _table_spec(tbl):
    return pl.BlockSpec(tbl.shape, lambda i: (0, 0), pipeline_mode=pl.Buffered(1))


def _peer_u_call(idx_parts, x2, gate_t, tbl, tb):
    t, d = x2.shape
    return pl.pallas_call(
        functools.partial(_peer_u_kernel, tb=tb),
        out_shape=jax.ShapeDtypeStruct((t, PEER_SLOTS), F32),
        grid=(t // tb,),
        in_specs=[pl.BlockSpec((tb // U_SPLIT, PEER_SLOTS), lambda i: (i, 0), memory_space=pltpu.SMEM)] * U_SPLIT
                 + [pl.BlockSpec((tb, d), lambda i: (i, 0)),
                    pl.BlockSpec((PEER_SLOTS, tb), lambda i: (0, i)),
                    _table_spec(tbl)],
        out_specs=pl.BlockSpec((tb, PEER_SLOTS), lambda i: (i, 0)),
        scratch_shapes=[pltpu.VMEM((U_SPLIT, PEER_SLOTS * ROW_SUBLANES, LANES), jnp.uint32),
                        pltpu.VMEM((U_SPLIT, PEER_SLOTS * ROW_SUBLANES, LANES), F32),
                        pltpu.VMEM((tb, SUBLANES, LANES), F32)],
        compiler_params=pltpu.CompilerParams(vmem_limit_bytes=VMEM_LIMIT),
        name="peer_u",
    )(*idx_parts, x2, gate_t, tbl)


def _peer_v_call(idx_parts, w_t, tbl, h1, mod3, g, seq, tb):
    t = w_t.shape[0]
    per_b = seq // tb
    return pl.pallas_call(
        functools.partial(_peer_v_kernel, tb=tb),
        out_shape=jax.ShapeDtypeStruct((t, D_MODEL), F32),
        grid=(t // tb,),
        in_specs=[pl.BlockSpec((tb // V_SPLIT, PEER_SLOTS), lambda i: (i, 0), memory_space=pltpu.SMEM)] * V_SPLIT
                 + [pl.BlockSpec((tb, PEER_SLOTS), lambda i: (i, 0)),
                    _table_spec(tbl),
                    pl.BlockSpec((tb, D_MODEL), lambda i: (i, 0)),
                    pl.BlockSpec((1, 6, D_MODEL), lambda i: (i // per_b, 0, 0)),
                    pl.BlockSpec((1, D_MODEL), lambda i: (0, 0))],
        out_specs=pl.BlockSpec((tb, D_MODEL), lambda i: (i, 0)),
        scratch_shapes=[pltpu.VMEM((2, PEER_SLOTS * ROW_SUBLANES, 2 * LANES), jnp.uint32),
                        pltpu.VMEM((2, PEER_SLOTS * ROW_SUBLANES, 2 * LANES), jnp.uint32),
                        pltpu.VMEM((tb, 2 * ROW_SUBLANES * PEER_SLOTS), F32),
                        pltpu.VMEM((tb, 2 * ROW_SUBLANES * PEER_SLOTS), F32),
                        pltpu.VMEM((tb, SUBLANES, LANES), F32)],
        compiler_params=pltpu.CompilerParams(vmem_limit_bytes=VMEM_LIMIT),
        name="peer_v",
    )(*idx_parts, w_t, tbl, h1, mod3, g.reshape(1, D_MODEL))


def _pad_lanes(v):
    return jnp.pad(v.reshape(1, -1), ((0, 0), (0, LANES - v.shape[-1])))


def kernel(x, c, ada_w, ada_b, norm1_g, w_in, ssd_conv_w, ssd_conv_b, ssd_dt_bias, ssd_a_log, ssd_d, ssd_norm_g, conf_dw_w, conf_dw_b, conf_ln_g, conf_ln_b, w_out, norm2_g, peer_w_query, peer_sub_keys, peer_u, peer_v, final_norm_g):
    bsz, seq, d = x.shape
    assert d == D_MODEL and ada_w.shape[0] == 1
    t = bsz * seq
    tm = min(512, seq)
    ts = min(256, seq)
    tr = min(256, seq)
    tb = min(512, seq)
    x2 = x.reshape(t, d)

    mod3 = _mod_call(c, ada_w[0], ada_b[0]).reshape(bsz, 6, d)

    wi = w_in[0]
    o1 = SSD_WIDTH
    o2 = o1 + SSD_XBC
    o3 = o2 + SSD_HEADS
    wz = wi[:, :o1].astype(BF16)
    wx = wi[:, o1:o2].astype(BF16)
    wd = jnp.pad(wi[:, o2:o3], ((0, 0), (0, LANES - SSD_HEADS))).astype(BF16)
    wg = wi[:, o3:].astype(BF16)
    z, xbc, glu, dt = _inproj_call(x2, mod3, norm1_g[0], wz, wx, wg, wd, seq, tm)

    h1, hn2 = _mixer_call(
        x2, z, xbc, glu, dt, mod3,
        ssd_conv_w[0], ssd_conv_b[0].reshape(1, -1), _pad_lanes(ssd_dt_bias[0]), _pad_lanes(ssd_a_log[0]),
        jnp.repeat(ssd_d[0], SSD_HEAD_DIM).reshape(1, -1), ssd_norm_g[0].reshape(1, -1),
        conf_dw_w[0], conf_dw_b[0].reshape(1, -1), conf_ln_g[0].reshape(1, -1), conf_ln_b[0].reshape(1, -1),
        w_out[0].astype(BF16), norm2_g[0].reshape(1, -1), bsz, seq, ts)

    keys = peer_sub_keys[0].reshape(PEER_HEADS * 2, PEER_N_KEYS, PEER_D_HALF).astype(BF16)
    *idx_parts, gate_t = _route_call(hn2, peer_w_query[0].astype(BF16), keys, tr)

    w_t = _peer_u_call(idx_parts, hn2, gate_t, _pack_table(peer_u[0]), tb)
    out = _peer_v_call(idx_parts, w_t, _pack_table(peer_v[0]), h1, mod3, final_norm_g, seq, tb)
    return out.reshape(bsz, seq, d)
```

```python
import functools
import math

import jax
import jax.numpy as jnp
from jax import lax
from jax.experimental import pallas as pl
from jax.experimental.pallas import tpu as pltpu

F32 = jnp.float32
BF16 = jnp.bfloat16
HIGHEST = lax.Precision.HIGHEST

D_MODEL = 1024
CHUNK = 64
SSD_WIDTH = 512
SSD_HEADS = 8
SSD_HEAD_DIM = 64
SSD_GROUPS = 2
SSD_STATE = 128
SSD_CONV = 4
SSD_XBC = 1024
CONF_WIDTH = 512
CONF_CONV = 31
PEER_HEADS = 8
PEER_N_KEYS = 128
PEER_D_HALF = 128
PEER_TOPK = 16
PEER_SLOTS = PEER_HEADS * PEER_TOPK
NORM_EPS = 1e-6

LANES = 128
SUBLANES = 8
ROW_SUBLANES = D_MODEL // 2 // LANES
VMEM_LIMIT = 56 * 1024 * 1024

V_SPLIT = 8
U_SPLIT = V_SPLIT

XBC_TAIL = 8
GLU_TAIL = 32


def _silu(v):
    return v * jax.nn.sigmoid(v)


def _softplus(v):
    return jnp.maximum(v, 0.0) + jnp.log(1.0 + jnp.exp(-jnp.abs(v)))


def _bdot(a, b):
    return jnp.dot(a.astype(BF16), b.astype(BF16), preferred_element_type=F32)


def _mod_kernel(c_ref, w_ref, b_ref, o_ref):
    cond = _silu(c_ref[...])
    o_ref[...] = jnp.dot(cond, w_ref[...], precision=HIGHEST, preferred_element_type=F32) + b_ref[...]


def _mod_call(c, ada_w, ada_b):
    bsz, d = c.shape
    n = ada_w.shape[1]
    return pl.pallas_call(
        _mod_kernel,
        out_shape=jax.ShapeDtypeStruct((bsz, n), F32),
        grid=(n // d,),
        in_specs=[pl.BlockSpec((bsz, d), lambda i: (0, 0)),
                  pl.BlockSpec((d, d), lambda i: (0, i)),
                  pl.BlockSpec((1, d), lambda i: (0, i))],
        out_specs=pl.BlockSpec((bsz, d), lambda i: (0, i)),
        name="mod",
    )(c, ada_w, ada_b.reshape(1, n))


def _inproj_kernel(x_ref, mod_ref, g_ref, wz_ref, wx_ref, wg_ref, wd_ref,
                   z_ref, xbc_ref, glu_ref, dt_ref):
    x = x_ref[...]
    ms = jnp.mean(x * x, axis=-1, keepdims=True)
    y = x * lax.rsqrt(ms + NORM_EPS) * g_ref[...]
    sh = mod_ref[0, 0:1, :]
    sc = mod_ref[0, 1:2, :]
    hn = (y * (1.0 + sc) + sh).astype(BF16)
    z_ref[...] = jnp.dot(hn, wz_ref[...], preferred_element_type=F32)
    xbc_ref[...] = jnp.dot(hn, wx_ref[...], preferred_element_type=F32)
    glu_ref[...] = jnp.dot(hn, wg_ref[...], preferred_element_type=F32)
    dt_ref[...] = jnp.dot(hn, wd_ref[...], preferred_element_type=F32)


def _inproj_call(x2, mod3, norm1_g, wz, wx, wg, wd, seq, tm):
    t, d = x2.shape
    per_b = seq // tm
    const = lambda i: (0, 0)
    row = lambda i: (i, 0)
    return pl.pallas_call(
        _inproj_kernel,
        out_shape=(jax.ShapeDtypeStruct((t, SSD_WIDTH), F32),
                   jax.ShapeDtypeStruct((t, SSD_XBC), F32),
                   jax.ShapeDtypeStruct((t, 2 * CONF_WIDTH), F32),
                   jax.ShapeDtypeStruct((t, LANES), F32)),
        grid=(t // tm,),
        in_specs=[pl.BlockSpec((tm, d), row),
                  pl.BlockSpec((1, 6, d), lambda i: (i // per_b, 0, 0)),
                  pl.BlockSpec((1, d), const),
                  pl.BlockSpec(wz.shape, const),
                  pl.BlockSpec(wx.shape, const),
                  pl.BlockSpec(wg.shape, const),
                  pl.BlockSpec(wd.shape, const)],
        out_specs=(pl.BlockSpec((tm, SSD_WIDTH), row),
                   pl.BlockSpec((tm, SSD_XBC), row),
                   pl.BlockSpec((tm, 2 * CONF_WIDTH), row),
                   pl.BlockSpec((tm, LANES), row)),
        compiler_params=pltpu.CompilerParams(vmem_limit_bytes=VMEM_LIMIT),
        name="inproj",
    )(x2, mod3, norm1_g.reshape(1, d), wz, wx, wg, wd)


def _mixer_kernel(x_ref, n1g_ref, wz_ref, wx_ref, wg_ref, wd_ref, mod_ref,
                  cw_ref, cb_ref, dtb_ref, alog_ref, dexp_ref, sng_ref,
                  dww_ref, dwb_ref, lng_ref, lnb_ref, wout_ref, n2g_ref,
                  h1_ref, hn2_ref,
                  xext_ref, gext_ref, hst_ref, xc_ref, xdt_ref, acs_ref, eacs_ref,
                  acst_ref, bmt_ref, y_ref, gsh_ref, z_ref, xbc_ref, glu_ref, dt_ref, *, ts):
    nc = ts // CHUNK
    hw = SSD_WIDTH // SSD_GROUPS

    _inproj_kernel(x_ref, mod_ref, n1g_ref, wz_ref, wx_ref, wg_ref, wd_ref, z_ref, xbc_ref, glu_ref, dt_ref)

    @pl.when(pl.program_id(1) == 0)
    def _():
        xext_ref[0:XBC_TAIL, :] = jnp.zeros((XBC_TAIL, SSD_XBC), F32)
        gext_ref[0:GLU_TAIL, :] = jnp.zeros((GLU_TAIL, CONF_WIDTH), F32)
        hst_ref[...] = jnp.zeros(hst_ref.shape, F32)

    xext_ref[XBC_TAIL:XBC_TAIL + ts, :] = xbc_ref[...]
    acc = cb_ref[...] + cw_ref[0:1, :] * xext_ref[pl.ds(XBC_TAIL - SSD_CONV + 1, ts), :]
    for k in range(1, SSD_CONV):
        acc = acc + cw_ref[k:k + 1, :] * xext_ref[pl.ds(XBC_TAIL - SSD_CONV + 1 + k, ts), :]
    xext_ref[0:XBC_TAIL, :] = xext_ref[ts:ts + XBC_TAIL, :]
    xc_ref[...] = _silu(acc)

    dt = _softplus(dt_ref[...] + dtb_ref[...])
    dta = dt * (-jnp.exp(alog_ref[...]))
    ri = lax.broadcasted_iota(jnp.int32, (ts, ts), 0)
    ci = lax.broadcasted_iota(jnp.int32, (ts, ts), 1)
    ltri = jnp.where(((ri // CHUNK) == (ci // CHUNK)) & (ci <= ri), 1.0, 0.0).astype(F32)
    acs = jnp.dot(ltri, dta, precision=HIGHEST, preferred_element_type=F32)
    acst_ref[...] = acs.T
    er = lax.broadcasted_iota(jnp.int32, (LANES, SSD_WIDTH), 0)
    ec = lax.broadcasted_iota(jnp.int32, (LANES, SSD_WIDTH), 1)
    expand = jnp.where((ec // SSD_HEAD_DIM) == er, 1.0, 0.0).astype(F32)
    dt_exp = jnp.dot(dt, expand, precision=HIGHEST, preferred_element_type=F32)
    acs_exp = jnp.dot(acs, expand, precision=HIGHEST, preferred_element_type=F32)
    acs_ref[...] = acs_exp
    eacs_ref[...] = jnp.exp(acs_exp)
    xdt_ref[...] = xc_ref[:, 0:SSD_WIDTH] * dt_exp
    bmt_ref[...] = xc_ref[:, SSD_WIDTH:SSD_WIDTH + SSD_GROUPS * SSD_STATE].T

    tr = lax.broadcasted_iota(jnp.int32, (CHUNK, CHUNK), 0)
    tc = lax.broadcasted_iota(jnp.int32, (CHUNK, CHUNK), 1)
    tril = tc <= tr
    c_off = SSD_WIDTH + SSD_GROUPS * SSD_STATE

    for c in range(nc):
        r0 = c * CHUNK
        rows = slice(r0, r0 + CHUNK)
        a_last = acs_ref[r0 + CHUNK - 1:r0 + CHUNK, :]
        xw = xdt_ref[rows, :] * jnp.exp(a_last - acs_ref[rows, :])
        cdec = jnp.exp(a_last)
        y_parts = []
        for g in range(SSD_GROUPS):
            cg = xc_ref[rows, c_off + g * SSD_STATE:c_off + (g + 1) * SSD_STATE].astype(BF16)
            bg = xc_ref[rows, SSD_WIDTH + g * SSD_STATE:SSD_WIDTH + (g + 1) * SSD_STATE].astype(BF16)
            cb = lax.dot_general(cg, bg, (((1,), (1,)), ((), ())), preferred_element_type=F32)
            hg = hst_ref[g]
            yoff = jnp.dot(cg, hg.astype(BF16), preferred_element_type=F32)
            st = _bdot(bmt_ref[g * SSD_STATE:(g + 1) * SSD_STATE, rows], xw[:, g * hw:(g + 1) * hw])
            hst_ref[g] = hg * cdec[:, g * hw:(g + 1) * hw] + st
            yds = []
            for hh in range(SSD_HEADS // SSD_GROUPS):
                h = g * (SSD_HEADS // SSD_GROUPS) + hh
                cols = slice(h * SSD_HEAD_DIM, (h + 1) * SSD_HEAD_DIM)
                seg = acs_ref[rows, cols] - acst_ref[h:h + 1, rows]
                dec = jnp.exp(jnp.where(tril, seg, -jnp.inf))
                yds.append(_bdot(cb * dec, xdt_ref[rows, cols]))
            y_parts.append(jnp.concatenate(yds, axis=1) + yoff * eacs_ref[rows, g * hw:(g + 1) * hw])
        y_ref[rows, :] = jnp.concatenate(y_parts, axis=1) + dexp_ref[...] * xc_ref[rows, 0:SSD_WIDTH]

    y = y_ref[...] * _silu(z_ref[...])
    y_ssd = y * lax.rsqrt(jnp.mean(y * y, axis=-1, keepdims=True) + NORM_EPS) * sng_ref[...]

    gext_ref[GLU_TAIL:GLU_TAIL + ts, :] = glu_ref[:, 0:CONF_WIDTH] * jax.nn.sigmoid(glu_ref[:, CONF_WIDTH:])
    span = ts + GLU_TAIL - SUBLANES
    for s in range(1, SUBLANES):
        gsh_ref[s - 1, 0:span, :] = gext_ref[pl.ds(s, span), :]
    u = dwb_ref[...]
    for k in range(CONF_CONV):
        off = GLU_TAIL - CONF_CONV + 1 + k
        s = off % SUBLANES
        rows = pl.ds(off - s, ts)
        u = u + dww_ref[k:k + 1, :] * (gext_ref[rows, :] if s == 0 else gsh_ref[s - 1, rows, :])
    gext_ref[0:GLU_TAIL, :] = gext_ref[ts:ts + GLU_TAIL, :]
    mu = jnp.mean(u, axis=-1, keepdims=True)
    uc = u - mu
    var = jnp.mean(uc * uc, axis=-1, keepdims=True)
    y_conf = _silu(uc * lax.rsqrt(var + NORM_EPS) * lng_ref[...] + lnb_ref[...])

    mix = (jnp.dot(y_ssd.astype(BF16), wout_ref[0:SSD_WIDTH, :], preferred_element_type=F32)
           + jnp.dot(y_conf.astype(BF16), wout_ref[SSD_WIDTH:, :], preferred_element_type=F32))
    h1 = x_ref[...] + mod_ref[0, 2:3, :] * mix
    h1_ref[...] = h1
    hn = h1 * lax.rsqrt(jnp.mean(h1 * h1, axis=-1, keepdims=True) + NORM_EPS) * n2g_ref[...]
    hn2_ref[...] = hn * (1.0 + mod_ref[0, 4:5, :]) + mod_ref[0, 3:4, :]


def _mixer_call(x2, n1g, wz, wx, wg, wd, mod3, cw, cb, dtb, alog, dexp, sng, dww, dwb, lng, lnb,
                wout, n2g, bsz, seq, ts):
    t, d = x2.shape
    per_b = seq // ts
    row = lambda b, j: (b * per_b + j, 0)
    const = lambda b, j: (0, 0)

    def full(a):
        return pl.BlockSpec(a.shape, const)

    return pl.pallas_call(
        functools.partial(_mixer_kernel, ts=ts),
        out_shape=(jax.ShapeDtypeStruct((t, d), F32), jax.ShapeDtypeStruct((t, d), F32)),
        grid=(bsz, per_b),
        in_specs=[pl.BlockSpec((ts, d), row),
                  full(n1g), full(wz), full(wx), full(wg), full(wd),
                  pl.BlockSpec((1, 6, d), lambda b, j: (b, 0, 0)),
                  full(cw), full(cb), full(dtb), full(alog), full(dexp), full(sng),
                  full(dww), full(dwb), full(lng), full(lnb), full(wout), full(n2g)],
        out_specs=(pl.BlockSpec((ts, d), row), pl.BlockSpec((ts, d), row)),
        scratch_shapes=[pltpu.VMEM((ts + XBC_TAIL, SSD_XBC), F32),
                        pltpu.VMEM((ts + GLU_TAIL, CONF_WIDTH), F32),
                        pltpu.VMEM((SSD_GROUPS, SSD_STATE, SSD_WIDTH // SSD_GROUPS), F32),
                        pltpu.VMEM((ts, SSD_XBC), F32),
                        pltpu.VMEM((ts, SSD_WIDTH), F32),
                        pltpu.VMEM((ts, SSD_WIDTH), F32),
                        pltpu.VMEM((ts, SSD_WIDTH), F32),
                        pltpu.VMEM((LANES, ts), F32),
                        pltpu.VMEM((SSD_GROUPS * SSD_STATE, ts), F32),
                        pltpu.VMEM((ts, SSD_WIDTH), F32),
                        pltpu.VMEM((SUBLANES - 1, ts + GLU_TAIL - SUBLANES, CONF_WIDTH), F32),
                        pltpu.VMEM((ts, SSD_WIDTH), F32),
                        pltpu.VMEM((ts, SSD_XBC), F32),
                        pltpu.VMEM((ts, 2 * CONF_WIDTH), F32),
                        pltpu.VMEM((ts, LANES), F32)],
        compiler_params=pltpu.CompilerParams(
            dimension_semantics=("arbitrary", "arbitrary"), vmem_limit_bytes=VMEM_LIMIT),
        name="mixer",
    )(x2, n1g, wz, wx, wg, wd, mod3, cw, cb, dtb, alog, dexp, sng, dww, dwb, lng, lnb, wout, n2g)


_PAIR_COUNTS = tuple(PEER_TOPK // (a + 1) for a in range(PEER_TOPK))


def _topk_rows(s, val_ref, pick_ref, payload=None):
    n = s.shape[0]
    h = n // 2
    iota = lax.broadcasted_iota(jnp.int32, (h, s.shape[1]), 0).astype(F32)
    a, b = s[:h], s[h:]
    first = a >= b
    hi, lo = jnp.where(first, a, b), jnp.where(first, b, a)
    ihi, ilo = jnp.where(first, iota, iota + float(h)), jnp.where(first, iota + float(h), iota)
    if payload is not None:
        phi, plo = jnp.where(first, payload[:h], payload[h:]), jnp.where(first, payload[h:], payload[:h])
    for r in range(val_ref.shape[0]):
        m = jnp.max(hi, axis=0, keepdims=True)
        am = jnp.min(jnp.where(hi == m, ihi, float(n)), axis=0, keepdims=True)
        val_ref[r:r + 1, :] = m
        hit = ihi == am
        if payload is None:
            pick_ref[r:r + 1, :] = am
        else:
            pick_ref[r:r + 1, :] = jnp.max(jnp.where(hit, phi, -1.0), axis=0, keepdims=True)
            phi = jnp.where(hit, plo, phi)
        hi = jnp.where(hit, lo, hi)
        ihi = jnp.where(hit, ilo, ihi)
        lo = jnp.where(hit, -jnp.inf, lo)


def _route_kernel(hn_ref, wq_ref, keys_ref, *refs):
    idx_refs = refs[:V_SPLIT]
    gate_ref, q_ref, topv_ref, topi_ref, best_ref, exp_ref, idx_scr = refs[V_SPLIT:]
    q_ref[...] = jnp.dot(hn_ref[...].astype(BF16), wq_ref[...], preferred_element_type=F32)
    nt = (((1,), (1,)), ((), ()))
    n_cand = sum(_PAIR_COUNTS)
    n_pad = -n_cand % (2 * SUBLANES)
    for lt in range(q_ref.shape[0] // LANES):
        toks = slice(lt * LANES, (lt + 1) * LANES)
        for h in range(PEER_HEADS):
            for i in range(2):
                col = (h * 2 + i) * PEER_D_HALF
                qh = q_ref[toks, col:col + PEER_D_HALF].astype(BF16)
                sc = lax.dot_general(keys_ref[h * 2 + i], qh, nt, preferred_element_type=F32)
                _topk_rows(sc, topv_ref.at[i], topi_ref.at[i])
            sv1, sv2 = topv_ref[0], topv_ref[1]
            si1, si2 = topi_ref[0], topi_ref[1]
            cand = jnp.concatenate([sv1[a:a + 1] + sv2[0:nb] for a, nb in enumerate(_PAIR_COUNTS)]
                                   + [jnp.full((n_pad, LANES), -jnp.inf, F32)], axis=0)
            cidx = jnp.concatenate([si1[a:a + 1] * float(PEER_N_KEYS) + si2[0:nb] for a, nb in enumerate(_PAIR_COUNTS)]
                                   + [jnp.zeros((n_pad, LANES), F32)], axis=0)
            _topk_rows(cand, best_ref, exp_ref.at[pl.ds(h * PEER_TOPK, PEER_TOPK)], payload=cidx)
            best = best_ref[...]
            e = jnp.exp(best - best[0:1, :])
            gate_ref[h * PEER_TOPK:(h + 1) * PEER_TOPK, toks] = e / jnp.sum(e, axis=0, keepdims=True)
        idx_scr[...] = (exp_ref[...].T * float(ROW_SUBLANES)).astype(jnp.int32)
        per = LANES // V_SPLIT
        for k in range(V_SPLIT):
            idx_refs[k][lt * per:(lt + 1) * per, :] = idx_scr[pl.ds(k, per, stride=V_SPLIT), :]


def _route_call(hn2, wq, keys, tm):
    t, d = hn2.shape
    return pl.pallas_call(
        _route_kernel,
        out_shape=[jax.ShapeDtypeStruct((t // V_SPLIT, PEER_SLOTS), jnp.int32)] * V_SPLIT
                  + [jax.ShapeDtypeStruct((PEER_SLOTS, t), F32)],
        grid=(t // tm,),
        in_specs=[pl.BlockSpec((tm, d), lambda i: (i, 0)),
                  pl.BlockSpec(wq.shape, lambda i: (0, 0)),
                  pl.BlockSpec(keys.shape, lambda i: (0, 0, 0))],
        out_specs=[pl.BlockSpec((tm // V_SPLIT, PEER_SLOTS), lambda i: (i, 0))] * V_SPLIT
                  + [pl.BlockSpec((PEER_SLOTS, tm), lambda i: (0, i))],
        scratch_shapes=[pltpu.VMEM((tm, wq.shape[1]), F32),
                        pltpu.VMEM((2, PEER_TOPK, LANES), F32),
                        pltpu.VMEM((2, PEER_TOPK, LANES), F32),
                        pltpu.VMEM((PEER_TOPK, LANES), F32),
                        pltpu.VMEM((PEER_SLOTS, LANES), F32),
                        pltpu.VMEM((LANES, PEER_SLOTS), jnp.int32)],
        compiler_params=pltpu.CompilerParams(vmem_limit_bytes=VMEM_LIMIT),
        name="route",
    )(hn2, wq, keys)


def _pack_kernel(t_ref, o_ref):
    half = t_ref.shape[1] // 2
    lo = pltpu.bitcast(t_ref[:, :half].astype(BF16).astype(F32), jnp.uint32)
    hi = pltpu.bitcast(t_ref[:, half:].astype(BF16).astype(F32), jnp.uint32)
    word = hi | (lo >> 16)
    rows = t_ref.shape[0]
    for s in range(ROW_SUBLANES):
        o_ref[pl.ds(s, rows, stride=ROW_SUBLANES), :] = word[:, s * LANES:(s + 1) * LANES]


def _pack_table(tbl):
    n, d = tbl.shape
    rows = 512
    return pl.pallas_call(
        _pack_kernel,
        out_shape=jax.ShapeDtypeStruct((n * ROW_SUBLANES, LANES), jnp.uint32),
        grid=(n // rows,),
        in_specs=[pl.BlockSpec((rows, d), lambda i: (i, 0))],
        out_specs=pl.BlockSpec((rows * ROW_SUBLANES, LANES), lambda i: (i, 0)),
        name="pack",
    )(tbl)


def _unpack_lo(w):
    return pltpu.bitcast(w << 16, F32)


def _unpack_hi(w):
    return pltpu.bitcast(w & jnp.uint32(0xFFFF0000), F32)


def _gather_rows(idx_refs, tbl_ref, dsts, p):
    for j in range(PEER_SLOTS):
        for idx_ref, (slot_ref, lane0) in zip(idx_refs, dsts):
            start = pl.multiple_of(idx_ref[p, j], ROW_SUBLANES)
            slot_ref[j * ROW_SUBLANES:(j + 1) * ROW_SUBLANES, lane0:lane0 + LANES] = (
                tbl_ref[pl.ds(start, ROW_SUBLANES), :])


def _peer_u_kernel(*refs, tb):
    idx_refs = refs[:U_SPLIT]
    x_ref, gate_ref, tbl_ref, w_ref, slot_ref, prod_ref, x3_ref = refs[U_SPLIT:]
    per_vreg = SUBLANES // ROW_SUBLANES
    lane = lax.broadcasted_iota(jnp.int32, (PEER_SLOTS, LANES), 1)
    for r in range(SUBLANES):
        x3_ref[:, r, :] = x_ref[:, r * LANES:(r + 1) * LANES]

    def products(slot_ref, prod_ref, t):
        xt = x3_ref[t]
        xlo = jnp.concatenate([xt[0:ROW_SUBLANES]] * per_vreg, axis=0)
        xhi = jnp.concatenate([xt[ROW_SUBLANES:]] * per_vreg, axis=0)
        words = slot_ref[...].reshape(PEER_SLOTS // per_vreg, SUBLANES, LANES)
        prod = _unpack_lo(words) * xlo[None] + _unpack_hi(words) * xhi[None]
        prod_ref[...] = prod.reshape(PEER_SLOTS * ROW_SUBLANES, LANES)

    def reduce_into(prod_ref, tl, acc):
        part = prod_ref[pl.ds(0, PEER_SLOTS, stride=ROW_SUBLANES), :]
        for r in range(1, ROW_SUBLANES):
            part = part + prod_ref[pl.ds(r, PEER_SLOTS, stride=ROW_SUBLANES), :]
        col = jnp.sum(part, axis=-1, keepdims=True)
        return jnp.where(lane == tl, col, acc)

    n = U_SPLIT
    prod_ref[...] = jnp.zeros(prod_ref.shape, F32)
    dsts = [(slot_ref.at[k], 0) for k in range(n)]
    for blk in range(tb // LANES):
        base = blk * LANES
        if blk == 0:
            _gather_rows(idx_refs, tbl_ref, dsts, 0)

        def body(i, acc):
            for k in range(n):
                acc = reduce_into(prod_ref.at[k], n * (i - 1) + k, acc)
            for k in range(n):
                products(slot_ref.at[k], prod_ref.at[k], base + n * i + k)
            _gather_rows(idx_refs, tbl_ref, dsts, jnp.minimum(base // n + i + 1, tb // n - 1))
            return acc

        act = lax.fori_loop(0, LANES // n, body, jnp.zeros((PEER_SLOTS, LANES), F32))
        for k in range(n):
            act = reduce_into(prod_ref.at[k], LANES - n + k, act)
        gelu = 0.5 * act * (1.0 + lax.erf(act * (1.0 / math.sqrt(2.0))))
        w_ref[blk * LANES:(blk + 1) * LANES, :] = (gate_ref[:, blk * LANES:(blk + 1) * LANES] * gelu).T


def _peer_v_kernel(*refs, tb):
    idx_refs = refs[:V_SPLIT]
    w_ref, tbl_ref, h1_ref, mod_ref, g_ref, o_ref, slot_a, slot_b, wrep_hi, wrep_lo, o3_ref = refs[V_SPLIT:]
    cols = 2 * ROW_SUBLANES * PEER_SLOTS
    w = w_ref[...]
    hi = w.astype(BF16)
    lo = (w - hi.astype(F32)).astype(BF16)
    jr = lax.broadcasted_iota(jnp.int32, (PEER_SLOTS, cols), 0)
    jc = lax.broadcasted_iota(jnp.int32, (PEER_SLOTS, cols), 1)
    expand = jnp.where(jc // (2 * ROW_SUBLANES) == jr, 1.0, 0.0).astype(BF16)
    wrep_hi[...] = jnp.dot(hi, expand, preferred_element_type=F32)
    wrep_lo[...] = jnp.dot(lo, expand, preferred_element_type=F32)
    rr = lax.broadcasted_iota(jnp.int32, (SUBLANES, cols), 0)
    rc = lax.broadcasted_iota(jnp.int32, (SUBLANES, cols), 1)
    mask = (rc % (2 * ROW_SUBLANES)) == 2 * (rr % ROW_SUBLANES) + rr // ROW_SUBLANES

    def lhs_rows(t):
        return [jnp.where(mask, jnp.broadcast_to(ref[pl.ds(t, 1), :], (SUBLANES, cols)), 0.0)
                for ref in (wrep_hi, wrep_lo)]

    def store_token(t, val):
        o3_ref[t] = val

    def combine_pair(slot_ref, t):
        lhs = jnp.concatenate(lhs_rows(t) + lhs_rows(t + 1), axis=0).astype(BF16)
        res = jnp.dot(lhs, pltpu.bitcast(slot_ref[...], BF16), preferred_element_type=F32)
        store_token(t, res[0:SUBLANES, 0:LANES] + res[SUBLANES:2 * SUBLANES, 0:LANES])
        store_token(t + 1, res[2 * SUBLANES:3 * SUBLANES, LANES:] + res[3 * SUBLANES:, LANES:])

    pairs = slot_a.shape[0]
    half = 2 * pairs
    per_trip = 2 * half
    assert per_trip == len(idx_refs)
    dsts = [((slot_a, slot_b)[k // half].at[(k // 2) % pairs], (k % 2) * LANES) for k in range(per_trip)]
    _gather_rows(idx_refs, tbl_ref, dsts, 0)

    def body(i, carry):
        t0 = per_trip * i
        for q in range(pairs):
            combine_pair(slot_a.at[q], t0 + 2 * q)
        for q in range(pairs):
            combine_pair(slot_b.at[q], t0 + half + 2 * q)
        _gather_rows(idx_refs, tbl_ref, dsts, jnp.minimum(i + 1, tb // per_trip - 1))
        return carry

    lax.fori_loop(0, tb // per_trip, body, 0)
    for r in range(SUBLANES):
        cols = slice(r * LANES, (r + 1) * LANES)
        o_ref[:, cols] = h1_ref[:, cols] + mod_ref[0, 5:6, cols] * o3_ref[:, r, :]
    h = o_ref[...]
    o_ref[...] = h * lax.rsqrt(jnp.mean(h * h, axis=-1, keepdims=True) + NORM_EPS) * g_ref[...]


def _table_spec(tbl):
    return pl.BlockSpec(tbl.shape, lambda i: (0, 0), pipeline_mode=pl.Buffered(1))


def _peer_u_call(idx_parts, x2, gate_t, tbl, tb):
    t, d = x2.shape
    return pl.pallas_call(
        functools.partial(_peer_u_kernel, tb=tb),
        out_shape=jax.ShapeDtypeStruct((t, PEER_SLOTS), F32),
        grid=(t // tb,),
        in_specs=[pl.BlockSpec((tb // U_SPLIT, PEER_SLOTS), lambda i: (i, 0), memory_space=pltpu.SMEM)] * U_SPLIT
                 + [pl.BlockSpec((tb, d), lambda i: (i, 0)),
                    pl.BlockSpec((PEER_SLOTS, tb), lambda i: (0, i)),
                    _table_spec(tbl)],
        out_specs=pl.BlockSpec((tb, PEER_SLOTS), lambda i: (i, 0)),
        scratch_shapes=[pltpu.VMEM((U_SPLIT, PEER_SLOTS * ROW_SUBLANES, LANES), jnp.uint32),
                        pltpu.VMEM((U_SPLIT, PEER_SLOTS * ROW_SUBLANES, LANES), F32),
                        pltpu.VMEM((tb, SUBLANES, LANES), F32)],
        compiler_params=pltpu.CompilerParams(vmem_limit_bytes=VMEM_LIMIT),
        name="peer_u",
    )(*idx_parts, x2, gate_t, tbl)


def _peer_v_call(idx_parts, w_t, tbl, h1, mod3, g, seq, tb):
    t = w_t.shape[0]
    per_b = seq // tb
    return pl.pallas_call(
        functools.partial(_peer_v_kernel, tb=tb),
        out_shape=jax.ShapeDtypeStruct((t, D_MODEL), F32),
        grid=(t // tb,),
        in_specs=[pl.BlockSpec((tb // V_SPLIT, PEER_SLOTS), lambda i: (i, 0), memory_space=pltpu.SMEM)] * V_SPLIT
                 + [pl.BlockSpec((tb, PEER_SLOTS), lambda i: (i, 0)),
                    _table_spec(tbl),
                    pl.BlockSpec((tb, D_MODEL), lambda i: (i, 0)),
                    pl.BlockSpec((1, 6, D_MODEL), lambda i: (i // per_b, 0, 0)),
                    pl.BlockSpec((1, D_MODEL), lambda i: (0, 0))],
        out_specs=pl.BlockSpec((tb, D_MODEL), lambda i: (i, 0)),
        scratch_shapes=[pltpu.VMEM((2, PEER_SLOTS * ROW_SUBLANES, 2 * LANES), jnp.uint32),
                        pltpu.VMEM((2, PEER_SLOTS * ROW_SUBLANES, 2 * LANES), jnp.uint32),
                        pltpu.VMEM((tb, 2 * ROW_SUBLANES * PEER_SLOTS), F32),
                        pltpu.VMEM((tb, 2 * ROW_SUBLANES * PEER_SLOTS), F32),
                        pltpu.VMEM((tb, SUBLANES, LANES), F32)],
        compiler_params=pltpu.CompilerParams(vmem_limit_bytes=VMEM_LIMIT),
        name="peer_v",
    )(*idx_parts, w_t, tbl, h1, mod3, g.reshape(1, D_MODEL))


def _pad_lanes(v):
    return jnp.pad(v.reshape(1, -1), ((0, 0), (0, LANES - v.shape[-1])))


def kernel(x, c, ada_w, ada_b, norm1_g, w_in, ssd_conv_w, ssd_conv_b, ssd_dt_bias, ssd_a_log, ssd_d, ssd_norm_g, conf_dw_w, conf_dw_b, conf_ln_g, conf_ln_b, w_out, norm2_g, peer_w_query, peer_sub_keys, peer_u, peer_v, final_norm_g):
    bsz, seq, d = x.shape
    assert d == D_MODEL and ada_w.shape[0] == 1
    t = bsz * seq
    tm = min(512, seq)
    ts = min(256, seq)
    tr = min(256, seq)
    tb = min(512, seq)
    x2 = x.reshape(t, d)

    mod3 = _mod_call(c, ada_w[0], ada_b[0]).reshape(bsz, 6, d)

    wi = w_in[0]
    o1 = SSD_WIDTH
    o2 = o1 + SSD_XBC
    o3 = o2 + SSD_HEADS
    wz = wi[:, :o1].astype(BF16)
    wx = wi[:, o1:o2].astype(BF16)
    wd = jnp.pad(wi[:, o2:o3], ((0, 0), (0, LANES - SSD_HEADS))).astype(BF16)
    wg = wi[:, o3:].astype(BF16)
    h1, hn2 = _mixer_call(
        x2, norm1_g[0].reshape(1, d), wz, wx, wg, wd, mod3,
        ssd_conv_w[0], ssd_conv_b[0].reshape(1, -1), _pad_lanes(ssd_dt_bias[0]), _pad_lanes(ssd_a_log[0]),
        jnp.repeat(ssd_d[0], SSD_HEAD_DIM).reshape(1, -1), ssd_norm_g[0].reshape(1, -1),
        conf_dw_w[0], conf_dw_b[0].reshape(1, -1), conf_ln_g[0].reshape(1, -1), conf_ln_b[0].reshape(1, -1),
        w_out[0].astype(BF16), norm2_g[0].reshape(1, -1), bsz, seq, ts)

    keys = peer_sub_keys[0].reshape(PEER_HEADS * 2, PEER_N_KEYS, PEER_D_HALF).astype(BF16)
    *idx_parts, gate_t = _route_call(hn2, peer_w_query[0].astype(BF16), keys, tr)

    w_t = _peer_u_call(idx_parts, hn2, gate_t, _pack_table(peer_u[0]), tb)
    out = _peer_v_call(idx_parts, w_t, _pack_table(peer_v[0]), h1, mod3, final_norm_g, seq, tb)
    return out.reshape(bsz, seq, d)
```
